```python
import jax, jax.numpy as jnp
from jax import lax
import numpy as np

D_MODEL = 1024
BATCH = 8
SEQ = 4096
DEPTH = 1

CTX_LEN = 256
GRID_W = 64
EPS = 1e-6
N_MOD = 6

A_WIDTH = D_MODEL
A_GROUPS = 8
A_GROUP_DIM = A_WIDTH // A_GROUPS
A_CHUNK = 128

B_HEADS = 4
B_DK = D_MODEL // 2 // B_HEADS
B_DV = D_MODEL // B_HEADS
B_QK_WIDTH = B_HEADS * B_DK
B_V_WIDTH = B_HEADS * B_DV
GATE_RANK = 16
GATE_TEMP = 16.0
GLA_CHUNK = 64

D_FF = ((8 * D_MODEL // 3 + 255) // 256) * 256

IN_SIZES = (A_WIDTH, A_WIDTH, B_QK_WIDTH, B_QK_WIDTH, B_V_WIDTH, B_V_WIDTH,
            GATE_RANK, GATE_RANK, D_MODEL, D_MODEL)

kernel_name = "hybrid_gmlp_gla_diffusion_block"


def _split_offsets():
    offs, acc = [], 0
    for s in IN_SIZES[:-1]:
        acc += s
        offs.append(acc)
    return offs


def _rmsnorm(x, g):
    xf = x.astype(jnp.float32)
    y = xf * lax.rsqrt(jnp.mean(xf * xf, axis=-1, keepdims=True) + EPS)
    return (y * g.astype(jnp.float32)).astype(x.dtype)


def _layernorm(x, g, b):
    xf = x.astype(jnp.float32)
    mu = jnp.mean(xf, axis=-1, keepdims=True)
    xc = xf - mu
    y = xc * lax.rsqrt(jnp.mean(xc * xc, axis=-1, keepdims=True) + EPS)
    return (y * g.astype(jnp.float32) + b.astype(jnp.float32)).astype(x.dtype)


def _modulate(xn, shift, scale):
    return xn * (1 + scale) + shift


def _chunk_mlp(u, v, n_chunks, ln_v_g, ln_v_b, w_spatial, b_spatial):
    B, T, _ = u.shape
    u = jax.nn.gelu(u, approximate=False)
    v = _layernorm(jax.nn.gelu(v, approximate=False), ln_v_g, ln_v_b)
    vc = v.reshape(B, n_chunks, A_CHUNK, A_GROUPS, A_GROUP_DIM)
    mixed = jnp.einsum('gpq,bnqgc->bnpgc', w_spatial, vc) + jnp.transpose(b_spatial)[:, :, None]
    return u * mixed.reshape(B, T, A_WIDTH)


def _gla_heads(q, k, v, af, ab, w_af, b_af, w_ab, b_ab):
    B, T, _ = q.shape
    f32 = jnp.float32
    q = q.astype(f32).reshape(B, T, B_HEADS, B_DK) * (B_DK ** -0.5)
    k = k.astype(f32).reshape(B, T, B_HEADS, B_DK)
    v = v.astype(f32).reshape(B, T, B_HEADS, B_DV)
    lf = (jax.nn.log_sigmoid((af @ w_af + b_af).astype(f32)) / GATE_TEMP).reshape(B, T, B_HEADS, B_DK)
    lb = (jax.nn.log_sigmoid((ab @ w_ab + b_ab).astype(f32)) / GATE_TEMP).reshape(B, T, B_HEADS, B_DK)
    return q, k, v, lf, lb


def _gla_scan(q, k, v, log_a, s0):
    B, T, H, DK = q.shape
    DV = v.shape[-1]
    n = T // GLA_CHUNK

    def to_chunks(t):
        return jnp.moveaxis(t.reshape(B, n, GLA_CHUNK, H, t.shape[-1]), 1, 0)

    mask = jnp.tril(jnp.ones((GLA_CHUNK, GLA_CHUNK), dtype=bool))

    def step(s, xs):
        qc, kc, vc, gc = xs
        b = jnp.cumsum(gc, axis=1)
        b_last = b[:, -1]
        q_dec = qc * jnp.exp(b)
        k_inv = kc * jnp.exp(-b)
        k_state = kc * jnp.exp(b_last[:, None] - b)
        o_inter = jnp.einsum('bchk,bhkv->bchv', q_dec, s)
        att = jnp.where(mask, jnp.einsum('bihk,bjhk->bhij', q_dec, k_inv), 0.0)
        o_intra = jnp.einsum('bhij,bjhv->bihv', att, vc)
        s_new = jnp.exp(b_last)[..., None] * s + jnp.einsum('bchk,bchv->bhkv', k_state, vc)
        return s_new, o_inter + o_intra

    s_fin, o = lax.scan(step, s0, (to_chunks(q), to_chunks(k), to_chunks(v), to_chunks(log_a)))
    o = jnp.moveaxis(o, 0, 1).reshape(B, T, H, DV)
    return o, s_fin


def _gla_out(o, r, gla_norm_g, dtype):
    B, T = o.shape[0], o.shape[1]
    o = o * lax.rsqrt(jnp.mean(o * o, axis=-1, keepdims=True) + EPS)
    o = o.reshape(B, T, B_V_WIDTH) * gla_norm_g.astype(jnp.float32)
    return o.astype(dtype) * jax.nn.silu(r)


def _token_mixer(h_lat, h_ctx, w_in, ln_v_g, ln_v_b, w_spatial, b_spatial,
                 w_alpha_f, b_alpha_f, w_alpha_b, b_alpha_b, gla_norm_g,
                 w_branch_a, w_branch_b, w_out, with_ctx_out):
    offs = _split_offsets()
    u_l, v_l, q_l, k_l, vv_l, r_l, af_l, ab_l, ga_l, gb_l = jnp.split(h_lat @ w_in, offs, axis=-1)
    u_c, v_c, q_c, k_c, vv_c, r_c, af_c, ab_c, ga_c, gb_c = jnp.split(h_ctx @ w_in, offs, axis=-1)
    B = h_lat.shape[0]

    ql, kl, vl, lfl, lbl = _gla_heads(q_l, k_l, vv_l, af_l, ab_l, w_alpha_f, b_alpha_f, w_alpha_b, b_alpha_b)
    qc, kc, vc, lfc, lbc = _gla_heads(q_c, k_c, vv_c, af_c, ab_c, w_alpha_f, b_alpha_f, w_alpha_b, b_alpha_b)
    s0 = jnp.zeros((B, B_HEADS, B_DK, B_DV), jnp.float32)
    fl = lambda t: jnp.flip(t, axis=1)
    o_cf, s_cf = _gla_scan(qc, kc, vc, lfc, s0)
    o_lf, _ = _gla_scan(ql, kl, vl, lfl, s_cf)
    o_cb, s_cb = _gla_scan(fl(qc), fl(kc), fl(vc), fl(lbc), s0)
    o_lb, _ = _gla_scan(fl(ql), fl(kl), fl(vl), fl(lbl), s_cb)
    b_lat = _gla_out(o_lf + fl(o_lb), r_l, gla_norm_g, h_lat.dtype)

    rows = h_lat.shape[1] // GRID_W
    a_lat = _chunk_mlp(u_l, v_l, rows // (A_CHUNK // GRID_W), ln_v_g, ln_v_b, w_spatial, b_spatial)

    def merge(a_out, b_out, ga, gb):
        y = jax.nn.sigmoid(ga) * (a_out @ w_branch_a) + jax.nn.sigmoid(gb) * (b_out @ w_branch_b)
        return y @ w_out

    out_lat = merge(a_lat, b_lat, ga_l, gb_l)
    if not with_ctx_out:
        return out_lat, None
    b_ctx = _gla_out(o_cf + fl(o_cb), r_c, gla_norm_g, h_ctx.dtype)
    a_ctx = _chunk_mlp(u_c, v_c, h_ctx.shape[1] // A_CHUNK, ln_v_g, ln_v_b, w_spatial, b_spatial)
    return out_lat, merge(a_ctx, b_ctx, ga_c, gb_c)


def _swiglu(h, w_ffn_in, w_ffn_out):
    a, g = jnp.split(h @ w_ffn_in, 2, axis=-1)
    return (jax.nn.silu(g) * a) @ w_ffn_out


def setup_inputs(seed: int = 0) -> dict:
    key = jax.random.key(seed)
    ks = jax.random.split(key, 24)
    f32 = jnp.float32
    nrm = lambda k, shape, s: jax.random.normal(k, shape, f32) * s
    L, D = DEPTH, D_MODEL
    return {
        "x": nrm(ks[0], (BATCH, SEQ, D), 1.0),
        "c": nrm(ks[1], (BATCH, D), 1.0),
        "ctx": nrm(ks[2], (BATCH, CTX_LEN, D), 1.0),
        "c_ctx": nrm(ks[3], (D,), 1.0),
        "w_ada": nrm(ks[4], (L, D, N_MOD * D), 0.5 * D ** -0.5),
        "b_ada": nrm(ks[5], (L, N_MOD * D), 0.02),
        "norm1_g": 1.0 + nrm(ks[6], (L, D), 0.02),
        "w_in": nrm(ks[7], (L, D, sum(IN_SIZES)), D ** -0.5),
        "ln_v_g": 1.0 + nrm(ks[8], (L, A_WIDTH), 0.02),
        "ln_v_b": nrm(ks[9], (L, A_WIDTH), 0.02),
        "w_spatial": nrm(ks[10], (L, A_GROUPS, A_CHUNK, A_CHUNK), A_CHUNK ** -0.5),
        "b_spatial": 1.0 + nrm(ks[11], (L, A_GROUPS, A_CHUNK), 0.02),
        "w_alpha_f": nrm(ks[12], (L, GATE_RANK, B_QK_WIDTH), GATE_RANK ** -0.5),
        "b_alpha_f": nrm(ks[13], (L, B_QK_WIDTH), 0.02),
        "w_alpha_b": nrm(ks[14], (L, GATE_RANK, B_QK_WIDTH), GATE_RANK ** -0.5),
        "b_alpha_b": nrm(ks[15], (L, B_QK_WIDTH), 0.02),
        "gla_norm_g": 1.0 + nrm(ks[16], (L, B_V_WIDTH), 0.02),
        "w_branch_a": nrm(ks[17], (L, A_WIDTH, D), A_WIDTH ** -0.5),
        "w_branch_b": nrm(ks[18], (L, B_V_WIDTH, D), B_V_WIDTH ** -0.5),
        "w_out": nrm(ks[19], (L, D, D), D ** -0.5),
        "norm2_g": 1.0 + nrm(ks[20], (L, D), 0.02),
        "w_ffn_in": nrm(ks[21], (L, D, 2 * D_FF), D ** -0.5),
        "w_ffn_out": nrm(ks[22], (L, D_FF, D), D_FF ** -0.5),
        "final_norm_g": 1.0 + nrm(ks[23], (D,), 0.02),
    }


def reference(x, c, ctx, c_ctx, w_ada, b_ada, norm1_g, w_in, ln_v_g, ln_v_b, w_spatial, b_spatial,
              w_alpha_f, b_alpha_f, w_alpha_b, b_alpha_b, gla_norm_g, w_branch_a, w_branch_b,
              w_out, norm2_g, w_ffn_in, w_ffn_out, final_norm_g):
    for l in range(DEPTH):
        update_ctx = l < DEPTH - 1
        mod = jnp.split((jax.nn.silu(c) @ w_ada[l] + b_ada[l])[:, None, :], N_MOD, axis=-1)
        mod_c = jnp.split(jax.nn.silu(c_ctx) @ w_ada[l] + b_ada[l], N_MOD, axis=-1)
        sh1, sc1, g1, sh2, sc2, g2 = mod
        sh1c, sc1c, g1c, sh2c, sc2c, g2c = mod_c

        h_lat = _modulate(_rmsnorm(x, norm1_g[l]), sh1, sc1)
        h_ctx = _modulate(_rmsnorm(ctx, norm1_g[l]), sh1c, sc1c)
        mix_l, mix_c = _token_mixer(h_lat, h_ctx, w_in[l], ln_v_g[l], ln_v_b[l], w_spatial[l], b_spatial[l],
                                    w_alpha_f[l], b_alpha_f[l], w_alpha_b[l], b_alpha_b[l], gla_norm_g[l],
                                    w_branch_a[l], w_branch_b[l], w_out[l], update_ctx)
        x = x + g1 * mix_l
        x = x + g2 * _swiglu(_modulate(_rmsnorm(x, norm2_g[l]), sh2, sc2), w_ffn_in[l], w_ffn_out[l])
        if update_ctx:
            ctx = ctx + g1c * mix_c
            ctx = ctx + g2c * _swiglu(_modulate(_rmsnorm(ctx, norm2_g[l]), sh2c, sc2c), w_ffn_in[l], w_ffn_out[l])
    return _rmsnorm(x, final_norm_g)
```

```python
import functools

import jax
import jax.numpy as jnp
from jax import lax
from jax.experimental import pallas as pl
from jax.experimental.pallas import tpu as pltpu

EPS = 1e-6
N_MOD = 6
A_GROUPS = 8
A_CHUNK = 128
B_HEADS = 4
B_DK = 128
B_DV = 256
QK_W = B_HEADS * B_DK
GATE_RANK = 16
GATE_TEMP = 16.0
GLA_CHUNK = 64
LANES = 128
V7X_VMEM_BYTES = 64 * 1024 * 1024

PROJ_TM = 512
TAIL_TM = 256
FF_CHUNK = 256

F32 = jnp.float32
BF16 = jnp.bfloat16


def _dot(a, b):
    return jnp.dot(a, b, preferred_element_type=F32)


def _resident(shape):
    nd = len(shape)
    return pl.BlockSpec(shape, lambda *_: (0,) * nd, pipeline_mode=pl.Buffered(1))


def _rmsnorm_f32(x, g):
    return x * lax.rsqrt(jnp.mean(x * x, axis=-1, keepdims=True) + EPS) * g


def _ada_kernel(c_ref, w_ref, b_ref, o_ref):
    c = c_ref[...]
    s = (c * jax.nn.sigmoid(c)).astype(BF16)
    o_ref[...] = _dot(s, w_ref[...].astype(BF16)) + b_ref[...]


def _ada(cc, w_ada, b_ada):
    rows, d = cc.shape
    n = w_ada.shape[1]
    bn = 512
    return pl.pallas_call(
        _ada_kernel,
        grid=(n // bn,),
        in_specs=[
            pl.BlockSpec((rows, d), lambda j: (0, 0)),
            pl.BlockSpec((d, bn), lambda j: (0, j)),
            pl.BlockSpec((1, bn), lambda j: (0, j)),
        ],
        out_specs=pl.BlockSpec((rows, bn), lambda j: (0, j)),
        out_shape=jax.ShapeDtypeStruct((rows, n), F32),
        name="ada",
    )(cc, w_ada, b_ada.reshape(1, n))


def _chunk_cumsum(g, reverse):
    n = g.shape[0]
    row = lax.broadcasted_iota(jnp.int32, g.shape, 0) % GLA_CHUNK
    s = 1
    while s < GLA_CHUNK:
        if reverse:
            shifted = pltpu.roll(g, n - s, axis=0)
            g = g + jnp.where(row < GLA_CHUNK - s, shifted, 0.0)
        else:
            shifted = pltpu.roll(g, s, axis=0)
            g = g + jnp.where(row >= s, shifted, 0.0)
        s *= 2
    return g


def _gelu(x):
    return 0.5 * x * (1.0 + lax.erf(x * (2.0 ** -0.5)))


def _log_sigmoid(z):
    return jnp.minimum(z, 0.0) - jnp.log1p(jnp.exp(-jnp.abs(z)))


def _proj_kernel(latent, d, x_ref, mod_ref, n1g_ref, w_ref, wab_ref, bab_ref,
                 lng_ref, lnb_ref, ws_ref, bsp_ref, wba_ref, *outs):
    if latent:
        (ya_ref, qf_ref, kf_ref, qb_ref, kb_ref, vv_ref, rs_ref, sgb_ref,
         af_ref, ab_ref, a_scr) = outs
    else:
        kf_ref, kb_ref, vv_ref, af_ref, ab_ref = outs
    tm = x_ref.shape[0]
    o_u, o_v, o_q, o_k, o_vv, o_r = 0, d, 2 * d, 2 * d + QK_W, 2 * d + 2 * QK_W, 3 * d + 2 * QK_W
    o_ga, o_gb, o_al = o_r + d, o_r + 2 * d, o_r + 3 * d

    x = x_ref[...]
    xn = _rmsnorm_f32(x, n1g_ref[...])
    hb = (xn * (1.0 + mod_ref[1:2, :]) + mod_ref[0:1, :]).astype(BF16)

    vv_ref[...] = _dot(hb, w_ref[:, o_vv:o_vv + d]).astype(BF16)
    k = _dot(hb, w_ref[:, o_k:o_k + QK_W])
    al = _dot(hb, w_ref[:, o_al:o_al + LANES]).astype(BF16)
    z = _dot(al, wab_ref[...]) + bab_ref[...]
    if latent:
        q = _dot(hb, w_ref[:, o_q:o_q + QK_W]) * (B_DK ** -0.5)
    for rev, k_ref, a_ref, q_ref in ((False, kf_ref, af_ref, qf_ref if latent else None),
                                     (True, kb_ref, ab_ref, qb_ref if latent else None)):
        off = QK_W if rev else 0
        for h in range(B_HEADS):
            cols = slice(h * B_DK, (h + 1) * B_DK)
            g = _log_sigmoid(z[:, off + h * B_DK: off + (h + 1) * B_DK]) / GATE_TEMP
            b = _chunk_cumsum(g, rev)
            k_ref[:, cols] = (k[:, cols] * jnp.exp(-b)).astype(BF16)
            if latent:
                q_ref[:, cols] = (q[:, cols] * jnp.exp(b)).astype(BF16)
            tot = jnp.sum(g.reshape(tm // GLA_CHUNK, GLA_CHUNK, B_DK), axis=1)
            a_ref[:, cols] = jnp.exp(tot)

    if not latent:
        return

    r = _dot(hb, w_ref[:, o_r:o_r + d])
    rs_ref[...] = (r * jax.nn.sigmoid(r)).astype(BF16)
    sgb_ref[...] = jax.nn.sigmoid(_dot(hb, w_ref[:, o_gb:o_gb + d])).astype(BF16)

    ug = _gelu(_dot(hb, w_ref[:, o_u:o_u + d]))
    vg = _gelu(_dot(hb, w_ref[:, o_v:o_v + d]))
    mu = jnp.mean(vg, axis=-1, keepdims=True)
    vc = vg - mu
    vn = (vc * lax.rsqrt(jnp.mean(vc * vc, axis=-1, keepdims=True) + EPS) * lng_ref[...]
          + lnb_ref[...]).astype(BF16)
    n_sp = tm // A_CHUNK
    gd = d // A_GROUPS
    for gi in range(A_GROUPS):
        gcols = slice(gi * gd, (gi + 1) * gd)
        rhs = jnp.concatenate([vn[c * A_CHUNK:(c + 1) * A_CHUNK, gcols] for c in range(n_sp)], axis=1)
        mixed = _dot(ws_ref[gi], rhs)
        for c in range(n_sp):
            rows = slice(c * A_CHUNK, (c + 1) * A_CHUNK)
            a_scr[rows, gcols] = (ug[rows, gcols]
                                  * (mixed[:, c * gd:(c + 1) * gd] + bsp_ref[:, gcols])).astype(BF16)
    ga = jax.nn.sigmoid(_dot(hb, w_ref[:, o_ga:o_ga + d]))
    ya_ref[...] = (ga * _dot(a_scr[...], wba_ref[...])).astype(BF16)


def _proj(latent, x, mod, n1g, w_all, wab, bab, lng, lnb, ws, bsp, wba, tm):
    bsz, t, d = x.shape
    nt = t // tm
    nch = tm // GLA_CHUNK
    tok = lambda w: pl.BlockSpec((None, tm, w), lambda b, i: (b, i, 0))
    dec = pl.BlockSpec((None, nch, QK_W), lambda b, i: (b, i, 0))
    sd = lambda w, dt: jax.ShapeDtypeStruct((bsz, t, w), dt)
    sdec = jax.ShapeDtypeStruct((bsz, t // GLA_CHUNK, QK_W), F32)
    if latent:
        out_specs = [tok(d), tok(QK_W), tok(QK_W), tok(QK_W), tok(QK_W), tok(d), tok(d), tok(d), dec, dec]
        out_shape = [sd(d, BF16), sd(QK_W, BF16), sd(QK_W, BF16), sd(QK_W, BF16), sd(QK_W, BF16),
                     sd(d, BF16), sd(d, BF16), sd(d, BF16), sdec, sdec]
        scratch = [pltpu.VMEM((tm, d), BF16)]
    else:
        out_specs = [tok(QK_W), tok(QK_W), tok(d), dec, dec]
        out_shape = [sd(QK_W, BF16), sd(QK_W, BF16), sd(d, BF16), sdec, sdec]
        scratch = []
    return pl.pallas_call(
        functools.partial(_proj_kernel, latent, d),
        grid=(bsz, nt),
        in_specs=[
            tok(d),
            pl.BlockSpec((None, 8, d), lambda b, i: (b, 0, 0)),
            _resident(n1g.shape), _resident(w_all.shape), _resident(wab.shape), _resident(bab.shape),
            _resident(lng.shape), _resident(lnb.shape), _resident(ws.shape), _resident(bsp.shape),
            _resident(wba.shape),
        ],
        out_specs=out_specs,
        out_shape=out_shape,
        scratch_shapes=scratch,
        compiler_params=pltpu.CompilerParams(
            dimension_semantics=("parallel", "parallel"),
            vmem_limit_bytes=V7X_VMEM_BYTES * 7 // 8),
        name="proj_lat" if latent else "proj_ctx",
    )(x, mod, n1g, w_all, wab, bab, lng, lnb, ws, bsp, wba)


GLA_UNROLL = 8


def _gla_kernel(t, tc, qf_ref, kf_ref, qb_ref, kb_ref, v_ref, rs_ref, af_ref, ab_ref,
                kcf_ref, kcb_ref, vc_ref, acf_ref, acb_ref, gain_ref, o_ref, part_ref):
    c = GLA_CHUNK
    n_chunks = t // c
    n_groups = n_chunks // GLA_UNROLL
    ri = lax.broadcasted_iota(jnp.int32, (c, c), 0)
    ci = lax.broadcasted_iota(jnp.int32, (c, c), 1)
    mask_f = ri >= ci
    mask_b = ri <= ci

    def kv(ki, vv):
        return lax.dot_general(ki, vv, (((0,), (0,)), ((), ())), preferred_element_type=F32)

    def bcast(col):
        return jnp.broadcast_to(col, (B_DK, B_DV))

    s_f = jnp.zeros((B_DK, B_DV), F32)
    s_b = jnp.zeros((B_DK, B_DV), F32)
    ncc = tc // c
    for j in range(ncc):
        rows = slice(j * c, (j + 1) * c)
        s_f = bcast(acf_ref[:, j:j + 1]) * (s_f + kv(kcf_ref[rows, :], vc_ref[rows, :]))
        jb = ncc - 1 - j
        rows = slice(jb * c, (jb + 1) * c)
        s_b = bcast(acb_ref[:, jb:jb + 1]) * (s_b + kv(kcb_ref[rows, :], vc_ref[rows, :]))

    def chunk(q_ref, k_ref, a_col, s, n, mask):
        rows = pl.ds(pl.multiple_of(n * c, c), c)
        qd = q_ref[rows, :]
        ki = k_ref[rows, :]
        vv = v_ref[rows, :]
        att = lax.dot_general(qd, ki, (((1,), (1,)), ((), ())), preferred_element_type=F32)
        att = jnp.where(mask, att, 0.0).astype(BF16)
        o = _dot(att, vv) + _dot(qd, s.astype(BF16))
        s = bcast(a_col) * (s + kv(ki, vv))
        return o, s, rows

    def finish(o, rows):
        tot = o + part_ref[rows, :]
        y = tot * lax.rsqrt(jnp.mean(tot * tot, axis=-1, keepdims=True) + EPS) * gain_ref[...]
        o_ref[rows, :] = (y * rs_ref[rows, :].astype(F32)).astype(o_ref.dtype)

    def make_body(second_half):
        def body(g, carry):
            s_f, s_b = carry
            gf = g + (n_groups // 2 if second_half else 0)
            gb = n_groups - 1 - gf
            for j in range(GLA_UNROLL):
                nf = gf * GLA_UNROLL + j
                jb = GLA_UNROLL - 1 - j
                nb = gb * GLA_UNROLL + jb
                o_f, s_f, rows_f = chunk(qf_ref, kf_ref, af_ref[gf, :, j:j + 1], s_f, nf, mask_f)
                o_b, s_b, rows_b = chunk(qb_ref, kb_ref, ab_ref[gb, :, jb:jb + 1], s_b, nb, mask_b)
                if second_half:
                    finish(o_f, rows_f)
                    finish(o_b, rows_b)
                else:
                    part_ref[rows_f, :] = o_f
                    part_ref[rows_b, :] = o_b
            return s_f, s_b
        return body

    carry = lax.fori_loop(0, n_groups // 2, make_body(False), (s_f, s_b))
    lax.fori_loop(0, n_groups // 2, make_body(True), carry)


def _gla(qf, kf, qb, kb, vv, rs, af_t, ab_t, kcf, kcb, vc, acf_t, acb_t, gain):
    bsz, t, d = vv.shape
    tc = vc.shape[1]
    seq = lambda n, w: pl.BlockSpec((None, n, w), lambda b, h: (b, 0, h))
    ng = af_t.shape[2]
    dec = pl.BlockSpec((None, None, ng, B_DK, GLA_UNROLL), lambda b, h: (b, h, 0, 0, 0))
    decc = pl.BlockSpec((None, None, B_DK, acf_t.shape[3]), lambda b, h: (b, h, 0, 0))
    return pl.pallas_call(
        functools.partial(_gla_kernel, t, tc),
        grid=(bsz, B_HEADS),
        in_specs=[seq(t, B_DK), seq(t, B_DK), seq(t, B_DK), seq(t, B_DK), seq(t, B_DV), seq(t, B_DV),
                  dec, dec, seq(tc, B_DK), seq(tc, B_DK), seq(tc, B_DV), decc, decc,
                  pl.BlockSpec((1, B_DV), lambda b, h: (0, h))],
        out_specs=seq(t, B_DV),
        out_shape=jax.ShapeDtypeStruct((bsz, t, d), BF16),
        scratch_shapes=[pltpu.VMEM((t, B_DV), F32)],
        compiler_params=pltpu.CompilerParams(
            dimension_semantics=("parallel", "parallel"),
            vmem_limit_bytes=V7X_VMEM_BYTES * 3 // 4),
        name="gla",
    )(qf, kf, qb, kb, vv, rs, af_t, ab_t, kcf, kcb, vc, acf_t, acb_t, gain)


def _tail_kernel(d_ff, x_ref, ya_ref, bl_ref, sgb_ref, mod_ref, wbb_ref, wo_ref, n2g_ref,
                 wfi_ref, wfo_ref, fng_ref, o_ref, f_scr):
    y = ya_ref[...].astype(F32) + sgb_ref[...].astype(F32) * _dot(bl_ref[...], wbb_ref[...])
    mix = _dot(y.astype(BF16), wo_ref[...])
    x1 = x_ref[...] + mod_ref[2:3, :] * mix
    h2 = (_rmsnorm_f32(x1, n2g_ref[...]) * (1.0 + mod_ref[4:5, :]) + mod_ref[3:4, :]).astype(BF16)
    for j in range(d_ff // FF_CHUNK):
        cols = slice(j * FF_CHUNK, (j + 1) * FF_CHUNK)
        a = _dot(h2, wfi_ref[:, cols])
        g = _dot(h2, wfi_ref[:, d_ff + j * FF_CHUNK: d_ff + (j + 1) * FF_CHUNK])
        f_scr[:, cols] = (g * jax.nn.sigmoid(g) * a).astype(BF16)
    x2 = x1 + mod_ref[5:6, :] * _dot(f_scr[...], wfo_ref[...])
    o_ref[...] = _rmsnorm_f32(x2, fng_ref[...])


def _tail(x, ya, bl, sgb, mod, wbb, wo, n2g, wfi, wfo, fng, tm):
    bsz, t, d = x.shape
    d_ff = wfo.shape[0]
    tok = pl.BlockSpec((None, tm, d), lambda b, i: (b, i, 0))
    return pl.pallas_call(
        functools.partial(_tail_kernel, d_ff),
        grid=(bsz, t // tm),
        in_specs=[tok, tok, tok, tok, pl.BlockSpec((None, 8, d), lambda b, i: (b, 0, 0)),
                  _resident(wbb.shape), _resident(wo.shape), _resident(n2g.shape),
                  _resident(wfi.shape), _resident(wfo.shape), _resident(fng.shape)],
        out_specs=tok,
        out_shape=jax.ShapeDtypeStruct((bsz, t, d), x.dtype),
        scratch_shapes=[pltpu.VMEM((tm, d_ff), BF16)],
        compiler_params=pltpu.CompilerParams(
            dimension_semantics=("parallel", "parallel"),
            vmem_limit_bytes=V7X_VMEM_BYTES * 7 // 8),
        name="tail",
    )(x, ya, bl, sgb, mod, wbb, wo, n2g, wfi, wfo, fng)


def _decay_table(a, unroll):
    bsz, n, _ = a.shape
    a = a.reshape(bsz, n // unroll, unroll, B_HEADS, B_DK)
    return jnp.transpose(a, (0, 3, 1, 4, 2))


def kernel(x, c, ctx, c_ctx, w_ada, b_ada, norm1_g, w_in, ln_v_g, ln_v_b, w_spatial, b_spatial,
           w_alpha_f, b_alpha_f, w_alpha_b, b_alpha_b, gla_norm_g, w_branch_a, w_branch_b,
           w_out, norm2_g, w_ffn_in, w_ffn_out, final_norm_g):
    bsz, t, d = x.shape
    tc = ctx.shape[1]
    depth = w_ada.shape[0]
    assert depth == 1 and d == B_HEADS * B_DV and d % A_GROUPS == 0
    assert t % (2 * GLA_UNROLL * GLA_CHUNK) == 0 and t % PROJ_TM == 0 and t % TAIL_TM == 0
    assert tc % A_CHUNK == 0 and tc % GLA_CHUNK == 0
    l = 0

    rows = -(-(bsz + 1) // 8) * 8
    cc = jnp.zeros((rows, d), F32).at[:bsz].set(c).at[bsz].set(c_ctx)
    mod_all = _ada(cc, w_ada[l], b_ada[l]).reshape(rows, N_MOD, d)
    mod_all = jnp.pad(mod_all, ((0, 0), (0, 8 - N_MOD), (0, 0)))
    mod_lat = mod_all[:bsz]
    mod_ctx = jnp.broadcast_to(mod_all[bsz:bsz + 1], (bsz, 8, d))

    w = w_in[l]
    o_al = 3 * d + 2 * QK_W + d
    o_ga = o_al + 2 * GATE_RANK
    w_all = jnp.concatenate(
        [w[:, :o_al], w[:, o_ga:], jnp.pad(w[:, o_al:o_ga], ((0, 0), (0, LANES - 2 * GATE_RANK)))],
        axis=1).astype(BF16)
    wab = jnp.zeros((LANES, 2 * QK_W), F32)
    wab = wab.at[:GATE_RANK, :QK_W].set(w_alpha_f[l]).at[GATE_RANK:2 * GATE_RANK, QK_W:].set(w_alpha_b[l])
    wab = wab.astype(BF16)
    bab = jnp.concatenate([b_alpha_f[l], b_alpha_b[l]]).reshape(1, 2 * QK_W)
    n1g = norm1_g[l].reshape(1, d)
    lng = ln_v_g[l].reshape(1, d)
    lnb = ln_v_b[l].reshape(1, d)
    ws = w_spatial[l].astype(BF16)
    bsp = jnp.repeat(jnp.transpose(b_spatial[l]), d // A_GROUPS, axis=1)
    wba = w_branch_a[l].astype(BF16)

    common = (n1g, w_all, wab, bab, lng, lnb, ws, bsp, wba)
    ya, qf, kf, qb, kb, vv, rs, sgb, af, ab = _proj(True, x, mod_lat, *common, tm=PROJ_TM)
    kcf, kcb, vc, acf, acb = _proj(False, ctx, mod_ctx, *common, tm=tc)

    af_t = _decay_table(af, GLA_UNROLL)
    ab_t = _decay_table(ab, GLA_UNROLL)
    ncc = tc // GLA_CHUNK
    acf_t = _decay_table(acf, ncc)[:, :, 0]
    acb_t = _decay_table(acb, ncc)[:, :, 0]
    bl = _gla(qf, kf, qb, kb, vv, rs, af_t, ab_t, kcf, kcb, vc, acf_t, acb_t,
              gla_norm_g[l].reshape(1, d))

    return _tail(x, ya, bl, sgb, mod_lat, w_branch_b[l].astype(BF16), w_out[l].astype(BF16),
                 norm2_g[l].reshape(1, d), w_ffn_in[l].astype(BF16), w_ffn_out[l].astype(BF16),
                 final_norm_g.reshape(1, d), tm=TAIL_TM)
```

```python
import functools

import jax
import jax.numpy as jnp
from jax import lax
from jax.experimental import pallas as pl
from jax.experimental.pallas import tpu as pltpu

EPS = 1e-6
N_MOD = 6
A_GROUPS = 8
A_CHUNK = 128
B_HEADS = 4
B_DK = 128
B_DV = 256
QK_W = B_HEADS * B_DK
GATE_RANK = 16
GATE_TEMP = 16.0
GLA_CHUNK = 64
GLA_GROUP = 4 * GLA_CHUNK
LANES = 128
V7X_VMEM_BYTES = 64 * 1024 * 1024

PROJ_TM = 512
TAIL_TM = 256
FF_CHUNK = 256

F32 = jnp.float32
BF16 = jnp.bfloat16


def _dot(a, b):
    return jnp.dot(a, b, preferred_element_type=F32)


def _resident(shape):
    nd = len(shape)
    return pl.BlockSpec(shape, lambda *_: (0,) * nd, pipeline_mode=pl.Buffered(1))


def _rmsnorm_f32(x, g):
    return x * lax.rsqrt(jnp.mean(x * x, axis=-1, keepdims=True) + EPS) * g


def _ada_kernel(c_ref, w_ref, b_ref, o_ref):
    c = c_ref[...]
    s = (c * jax.nn.sigmoid(c)).astype(BF16)
    o_ref[...] = _dot(s, w_ref[...].astype(BF16)) + b_ref[...]


def _ada(cc, w_ada, b_ada):
    rows, d = cc.shape
    n = w_ada.shape[1]
    bn = 512
    return pl.pallas_call(
        _ada_kernel,
        grid=(n // bn,),
        in_specs=[
            pl.BlockSpec((rows, d), lambda j: (0, 0)),
            pl.BlockSpec((d, bn), lambda j: (0, j)),
            pl.BlockSpec((1, bn), lambda j: (0, j)),
        ],
        out_specs=pl.BlockSpec((rows, bn), lambda j: (0, j)),
        out_shape=jax.ShapeDtypeStruct((rows, n), F32),
        name="ada",
    )(cc, w_ada, b_ada.reshape(1, n))


def _chunk_cumsum(g, reverse):
    n = g.shape[0]
    row = lax.broadcasted_iota(jnp.int32, g.shape, 0) % GLA_CHUNK
    s = 1
    while s < GLA_CHUNK:
        if reverse:
            shifted = pltpu.roll(g, n - s, axis=0)
            g = g + jnp.where(row < GLA_CHUNK - s, shifted, 0.0)
        else:
            shifted = pltpu.roll(g, s, axis=0)
            g = g + jnp.where(row >= s, shifted, 0.0)
        s *= 2
    return g


def _gelu(x):
    return 0.5 * x * (1.0 + lax.erf(x * (2.0 ** -0.5)))


def _log_sigmoid(z):
    return jnp.minimum(z, 0.0) - jnp.log1p(jnp.exp(-jnp.abs(z)))


def _proj_kernel(latent, d, x_ref, mod_ref, n1g_ref, w_ref, wab_ref, bab_ref,
                 lng_ref, lnb_ref, ws_ref, bsp_ref, wba_ref, *outs):
    if latent:
        (ya_ref, qf_ref, kf_ref, qb_ref, kb_ref, vv_ref, rs_ref, sgb_ref,
         af_ref, ab_ref, a_scr) = outs
    else:
        kf_ref, kb_ref, vv_ref, af_ref, ab_ref = outs
    tm = x_ref.shape[0]
    n_ch = tm // GLA_CHUNK
    cpg = GLA_GROUP // GLA_CHUNK
    o_u, o_v, o_q, o_k, o_vv, o_r = 0, d, 2 * d, 2 * d + QK_W, 2 * d + 2 * QK_W, 3 * d + 2 * QK_W
    o_ga, o_gb, o_al = o_r + d, o_r + 2 * d, o_r + 3 * d

    x = x_ref[...]
    xn = _rmsnorm_f32(x, n1g_ref[...])
    hb = (xn * (1.0 + mod_ref[1:2, :]) + mod_ref[0:1, :]).astype(BF16)

    vv_ref[...] = _dot(hb, w_ref[:, o_vv:o_vv + d]).astype(BF16)
    k = _dot(hb, w_ref[:, o_k:o_k + QK_W])
    al = _dot(hb, w_ref[:, o_al:o_al + LANES]).astype(BF16)
    z = _dot(al, wab_ref[...]) + bab_ref[...]
    if latent:
        q = _dot(hb, w_ref[:, o_q:o_q + QK_W]) * (B_DK ** -0.5)
    for rev, k_ref, a_ref, q_ref in ((False, kf_ref, af_ref, qf_ref if latent else None),
                                     (True, kb_ref, ab_ref, qb_ref if latent else None)):
        off = QK_W if rev else 0
        for h in range(B_HEADS):
            cols = slice(h * B_DK, (h + 1) * B_DK)
            g = _log_sigmoid(z[:, off + h * B_DK: off + (h + 1) * B_DK]) / GATE_TEMP
            b = _chunk_cumsum(g, rev)
            k_ref[cols, :] = jnp.transpose(k[:, cols] * jnp.exp(-b)).astype(BF16)
            if latent:
                q_ref[:, cols] = (q[:, cols] * jnp.exp(b)).astype(BF16)
            tot = jnp.sum(g.reshape(n_ch, GLA_CHUNK, B_DK), axis=1)
            tot = jnp.concatenate([tot, jnp.zeros((LANES - n_ch, B_DK), F32)], axis=0)
            a_t = jnp.exp(jnp.transpose(tot))
            for gi in range(tm // GLA_GROUP):
                a_ref[gi, cols, :] = a_t[:, gi * cpg:(gi + 1) * cpg]

    if not latent:
        return

    r = _dot(hb, w_ref[:, o_r:o_r + d])
    rs_ref[...] = (r * jax.nn.sigmoid(r)).astype(BF16)
    sgb_ref[...] = jax.nn.sigmoid(_dot(hb, w_ref[:, o_gb:o_gb + d])).astype(BF16)

    ug = _gelu(_dot(hb, w_ref[:, o_u:o_u + d]))
    vg = _gelu(_dot(hb, w_ref[:, o_v:o_v + d]))
    mu = jnp.mean(vg, axis=-1, keepdims=True)
    vc = vg - mu
    vn = (vc * lax.rsqrt(jnp.mean(vc * vc, axis=-1, keepdims=True) + EPS) * lng_ref[...]
          + lnb_ref[...]).astype(BF16)
    n_sp = tm // A_CHUNK
    gd = d // A_GROUPS
    for gi in range(A_GROUPS):
        gcols = slice(gi * gd, (gi + 1) * gd)
        rhs = jnp.concatenate([vn[c * A_CHUNK:(c + 1) * A_CHUNK, gcols] for c in range(n_sp)], axis=1)
        mixed = _dot(ws_ref[gi], rhs)
        for c in range(n_sp):
            rows = slice(c * A_CHUNK, (c + 1) * A_CHUNK)
            a_scr[rows, gcols] = (ug[rows, gcols]
                                  * (mixed[:, c * gd:(c + 1) * gd] + bsp_ref[:, gcols])).astype(BF16)
    ga = jax.nn.sigmoid(_dot(hb, w_ref[:, o_ga:o_ga + d]))
    ya_ref[...] = (ga * _dot(a_scr[...], wba_ref[...])).astype(BF16)


def _proj(latent, x, mod, n1g, w_all, wab, bab, lng, lnb, ws, bsp, wba, tm):
    bsz, t, d = x.shape
    nt = t // tm
    tok = lambda w: pl.BlockSpec((None, tm, w), lambda b, i: (b, i, 0))
    ktr = pl.BlockSpec((None, QK_W, tm), lambda b, i: (b, 0, i))
    gpt = tm // GLA_GROUP
    cpg = GLA_GROUP // GLA_CHUNK
    dec = pl.BlockSpec((None, gpt, QK_W, cpg), lambda b, i: (b, i, 0, 0))
    sd = lambda w, dt: jax.ShapeDtypeStruct((bsz, t, w), dt)
    sktr = jax.ShapeDtypeStruct((bsz, QK_W, t), BF16)
    sdec = jax.ShapeDtypeStruct((bsz, t // GLA_GROUP, QK_W, cpg), F32)
    if latent:
        out_specs = [tok(d), tok(QK_W), ktr, tok(QK_W), ktr, tok(d), tok(d), tok(d), dec, dec]
        out_shape = [sd(d, BF16), sd(QK_W, BF16), sktr, sd(QK_W, BF16), sktr,
                     sd(d, BF16), sd(d, BF16), sd(d, BF16), sdec, sdec]
        scratch = [pltpu.VMEM((tm, d), BF16)]
    else:
        out_specs = [ktr, ktr, tok(d), dec, dec]
        out_shape = [sktr, sktr, sd(d, BF16), sdec, sdec]
        scratch = []
    return pl.pallas_call(
        functools.partial(_proj_kernel, latent, d),
        grid=(bsz, nt),
        in_specs=[
            tok(d),
            pl.BlockSpec((None, 8, d), lambda b, i: (b, 0, 0)),
            _resident(n1g.shape), _resident(w_all.shape), _resident(wab.shape), _resident(bab.shape),
            _resident(lng.shape), _resident(lnb.shape), _resident(ws.shape), _resident(bsp.shape),
            _resident(wba.shape),
        ],
        out_specs=out_specs,
        out_shape=out_shape,
        scratch_shapes=scratch,
        compiler_params=pltpu.CompilerParams(
            dimension_semantics=("parallel", "parallel"),
            vmem_limit_bytes=V7X_VMEM_BYTES * 7 // 8),
        name="proj_lat" if latent else "proj_ctx",
    )(x, mod, n1g, w_all, wab, bab, lng, lnb, ws, bsp, wba)


def _gla_kernel(t, tc, qf_ref, ktf_ref, qb_ref, ktb_ref, v_ref, rs_ref, af_ref, ab_ref,
                ktcf_ref, ktcb_ref, vc_ref, acf_ref, acb_ref, gain_ref, o_ref,
                part_ref, att_scr, oin_scr, x_scr):
    c = GLA_CHUNK
    gsz = GLA_GROUP
    cpg = gsz // c
    n_g = t // gsz
    r = lax.broadcasted_iota(jnp.int32, (gsz, gsz), 0)
    col = lax.broadcasted_iota(jnp.int32, (gsz, gsz), 1)
    blk = r - r % c

    def bcast(a_col):
        return jnp.broadcast_to(a_col, (B_DK, B_DV))

    def chunk_rows(v, n):
        parts = []
        if n:
            parts.append(jnp.zeros((n * c, v.shape[1]), v.dtype))
        parts.append(v[n * c:(n + 1) * c])
        if n < cpg - 1:
            parts.append(jnp.zeros(((cpg - 1 - n) * c, v.shape[1]), v.dtype))
        return jnp.concatenate(parts, axis=0)

    class Direction:
        def __init__(self, idx, q_ref, kt_ref, a_ref, reverse):
            self.idx, self.q_ref, self.kt_ref, self.a_ref, self.reverse = idx, q_ref, kt_ref, a_ref, reverse
            self.order = range(cpg - 1, -1, -1) if reverse else range(cpg)

        def group(self, p):
            return (n_g - 1 - p) if self.reverse else p

        def rows(self, p):
            return pl.ds(pl.multiple_of(self.group(p) * gsz, gsz), gsz)

        def scores(self, p, slot):
            rows = self.rows(p)
            att = _dot(self.q_ref[rows, :], self.kt_ref[:, rows])
            if self.reverse:
                att = jnp.where(col >= r, jnp.where(col < blk + c, att, 0.0), 0.0)
            else:
                att = jnp.where(col <= r, jnp.where(col >= blk, att, 0.0), 0.0)
            att_scr[self.idx, slot] = att.astype(BF16)

        def values(self, p, slot):
            rows = self.rows(p)
            v = v_ref[rows, :]
            kt = self.kt_ref[:, rows]
            oin_scr[self.idx, slot] = _dot(att_scr[self.idx, slot], v)
            for n in range(cpg):
                x_scr[self.idx, slot, n] = _dot(kt, chunk_rows(v, n))

        def state(self, p, slot, s, finish):
            rows = self.rows(p)
            g = self.group(p)
            q = self.q_ref[rows, :]
            o_inter = [None] * cpg
            for n in self.order:
                o_inter[n] = _dot(q[n * c:(n + 1) * c], s.astype(BF16))
                s = bcast(self.a_ref[g, :, n:n + 1]) * (s + x_scr[self.idx, slot, n])
            o = oin_scr[self.idx, slot] + jnp.concatenate(o_inter, axis=0)
            if finish:
                tot = o + part_ref[rows, :]
                y = tot * lax.rsqrt(jnp.mean(tot * tot, axis=-1, keepdims=True) + EPS) * gain_ref[...]
                o_ref[rows, :] = (y * rs_ref[rows, :].astype(F32)).astype(o_ref.dtype)
            else:
                part_ref[rows, :] = o
            return s

    dirs = (Direction(0, qf_ref, ktf_ref, af_ref, False), Direction(1, qb_ref, ktb_ref, ab_ref, True))

    states = []
    ncg = tc // gsz
    for d, ktc_ref, ac_ref in zip(dirs, (ktcf_ref, ktcb_ref), (acf_ref, acb_ref)):
        s = jnp.zeros((B_DK, B_DV), F32)
        for gi in (range(ncg - 1, -1, -1) if d.reverse else range(ncg)):
            rows = slice(gi * gsz, (gi + 1) * gsz)
            kt = ktc_ref[:, rows]
            v = vc_ref[rows, :]
            for n in d.order:
                s = bcast(ac_ref[gi, :, n:n + 1]) * (s + _dot(kt, chunk_rows(v, n)))
        states.append(s)

    for d in dirs:
        d.scores(0, 0)
        d.scores(1, 1)
    for d in dirs:
        d.values(0, 0)

    def make_body(finish, base):
        def body(i, states):
            for u in range(2):
                p = base + 2 * i + u
                for d in dirs:
                    d.scores(p + 2, u)
                for d in dirs:
                    d.values(p + 1, 1 - u)
                states = tuple(d.state(p, u, s, finish) for d, s in zip(dirs, states))
            return states
        return body

    half = n_g // 2
    states = lax.fori_loop(0, half // 2, make_body(False, 0), tuple(states))
    states = lax.fori_loop(0, (half - 2) // 2, make_body(True, half), states)

    for d in dirs:
        d.values(n_g - 1, 1)
    states = [d.state(n_g - 2, 0, s, True) for d, s in zip(dirs, states)]
    for d, s in zip(dirs, states):
        d.state(n_g - 1, 1, s, True)


def _gla(qf, ktf, qb, ktb, vv, rs, af, ab, ktcf, ktcb, vc, acf, acb, gain):
    bsz, t, d = vv.shape
    tc = vc.shape[1]
    seq = lambda n, w: pl.BlockSpec((None, n, w), lambda b, h: (b, 0, h))
    seqt = lambda n: pl.BlockSpec((None, B_DK, n), lambda b, h: (b, h, 0))
    dec = lambda a: pl.BlockSpec((None, a.shape[1], B_DK, a.shape[3]), lambda b, h: (b, 0, h, 0))
    return pl.pallas_call(
        functools.partial(_gla_kernel, t, tc),
        grid=(bsz, B_HEADS),
        in_specs=[seq(t, B_DK), seqt(t), seq(t, B_DK), seqt(t), seq(t, B_DV), seq(t, B_DV),
                  dec(af), dec(ab), seqt(tc), seqt(tc), seq(tc, B_DV), dec(acf), dec(acb),
                  pl.BlockSpec((1, B_DV), lambda b, h: (0, h))],
        out_specs=seq(t, B_DV),
        out_shape=jax.ShapeDtypeStruct((bsz, t, d), BF16),
        scratch_shapes=[pltpu.VMEM((t, B_DV), F32),
                        pltpu.VMEM((2, 2, GLA_GROUP, GLA_GROUP), BF16),
                        pltpu.VMEM((2, 2, GLA_GROUP, B_DV), F32),
                        pltpu.VMEM((2, 2, GLA_GROUP // GLA_CHUNK, B_DK, B_DV), F32)],
        compiler_params=pltpu.CompilerParams(
            dimension_semantics=("parallel", "parallel"),
            vmem_limit_bytes=V7X_VMEM_BYTES * 3 // 4),
        name="gla",
    )(qf, ktf, qb, ktb, vv, rs, af, ab, ktcf, ktcb, vc, acf, acb, gain)


def _tail_kernel(d_ff, x_ref, ya_ref, bl_ref, sgb_ref, mod_ref, wbb_ref, wo_ref, n2g_ref,
                 wfi_ref, wfo_ref, fng_ref, o_ref, f_scr):
    y = ya_ref[...].astype(F32) + sgb_ref[...].astype(F32) * _dot(bl_ref[...], wbb_ref[...])
    mix = _dot(y.astype(BF16), wo_ref[...])
    x1 = x_ref[...] + mod_ref[2:3, :] * mix
    h2 = (_rmsnorm_f32(x1, n2g_ref[...]) * (1.0 + mod_ref[4:5, :]) + mod_ref[3:4, :]).astype(BF16)
    for j in range(d_ff // FF_CHUNK):
        cols = slice(j * FF_CHUNK, (j + 1) * FF_CHUNK)
        a = _dot(h2, wfi_ref[:, cols])
        g = _dot(h2, wfi_ref[:, d_ff + j * FF_CHUNK: d_ff + (j + 1) * FF_CHUNK])
        f_scr[:, cols] = (g * jax.nn.sigmoid(g) * a).astype(BF16)
    x2 = x1 + mod_ref[5:6, :] * _dot(f_scr[...], wfo_ref[...])
    o_ref[...] = _rmsnorm_f32(x2, fng_ref[...])


def _tail(x, ya, bl, sgb, mod, wbb, wo, n2g, wfi, wfo, fng, tm):
    bsz, t, d = x.shape
    d_ff = wfo.shape[0]
    tok = pl.BlockSpec((None, tm, d), lambda b, i: (b, i, 0))
    return pl.pallas_call(
        functools.partial(_tail_kernel, d_ff),
        grid=(bsz, t // tm),
        in_specs=[tok, tok, tok, tok, pl.BlockSpec((None, 8, d), lambda b, i: (b, 0, 0)),
                  _resident(wbb.shape), _resident(wo.shape), _resident(n2g.shape),
                  _resident(wfi.shape), _resident(wfo.shape), _resident(fng.shape)],
        out_specs=tok,
        out_shape=jax.ShapeDtypeStruct((bsz, t, d), x.dtype),
        scratch_shapes=[pltpu.VMEM((tm, d_ff), BF16)],
        compiler_params=pltpu.CompilerParams(
            dimension_semantics=("parallel", "parallel"),
            vmem_limit_bytes=V7X_VMEM_BYTES * 7 // 8),
        name="tail",
    )(x, ya, bl, sgb, mod, wbb, wo, n2g, wfi, wfo, fng)


def kernel(x, c, ctx, c_ctx, w_ada, b_ada, norm1_g, w_in, ln_v_g, ln_v_b, w_spatial, b_spatial,
           w_alpha_f, b_alpha_f, w_alpha_b, b_alpha_b, gla_norm_g, w_branch_a, w_branch_b,
           w_out, norm2_g, w_ffn_in, w_ffn_out, final_norm_g):
    bsz, t, d = x.shape
    tc = ctx.shape[1]
    depth = w_ada.shape[0]
    assert depth == 1 and d == B_HEADS * B_DV and d % A_GROUPS == 0
    assert t % (4 * GLA_GROUP) == 0 and t % PROJ_TM == 0 and PROJ_TM % GLA_GROUP == 0 and t % TAIL_TM == 0
    assert tc % A_CHUNK == 0 and tc % GLA_GROUP == 0
    l = 0

    rows = -(-(bsz + 1) // 8) * 8
    cc = jnp.zeros((rows, d), F32).at[:bsz].set(c).at[bsz].set(c_ctx)
    mod_all = _ada(cc, w_ada[l], b_ada[l]).reshape(rows, N_MOD, d)
    mod_all = jnp.pad(mod_all, ((0, 0), (0, 8 - N_MOD), (0, 0)))
    mod_lat = mod_all[:bsz]
    mod_ctx = jnp.broadcast_to(mod_all[bsz:bsz + 1], (bsz, 8, d))

    w = w_in[l]
    o_al = 3 * d + 2 * QK_W + d
    o_ga = o_al + 2 * GATE_RANK
    w_all = jnp.concatenate(
        [w[:, :o_al], w[:, o_ga:], jnp.pad(w[:, o_al:o_ga], ((0, 0), (0, LANES - 2 * GATE_RANK)))],
        axis=1).astype(BF16)
    wab = jnp.zeros((LANES, 2 * QK_W), F32)
    wab = wab.at[:GATE_RANK, :QK_W].set(w_alpha_f[l]).at[GATE_RANK:2 * GATE_RANK, QK_W:].set(w_alpha_b[l])
    wab = wab.astype(BF16)
    bab = jnp.concatenate([b_alpha_f[l], b_alpha_b[l]]).reshape(1, 2 * QK_W)
    n1g = norm1_g[l].reshape(1, d)
    lng = ln_v_g[l].reshape(1, d)
    lnb = ln_v_b[l].reshape(1, d)
    ws = w_spatial[l].astype(BF16)
    bsp = jnp.repeat(jnp.transpose(b_spatial[l]), d // A_GROUPS, axis=1)
    wba = w_branch_a[l].astype(BF16)

    common = (n1g, w_all, wab, bab, lng, lnb, ws, bsp, wba)
    ya, qf, ktf, qb, ktb, vv, rs, sgb, af, ab = _proj(True, x, mod_lat, *common, tm=PROJ_TM)
    ktcf, ktcb, vc, acf, acb = _proj(False, ctx, mod_ctx, *common, tm=tc)
    bl = _gla(qf, ktf, qb, ktb, vv, rs, af, ab, ktcf, ktcb, vc, acf, acb,
              gla_norm_g[l].reshape(1, d))

    return _tail(x, ya, bl, sgb, mod_lat, w_branch_b[l].astype(BF16), w_out[l].astype(BF16),
                 norm2_g[l].reshape(1, d), w_ffn_in[l].astype(BF16), w_ffn_out[l].astype(BF16),
                 final_norm_g.reshape(1, d), tm=TAIL_TM)
```

```python
import functools

import jax
import jax.numpy as jnp
from jax import lax
from jax.experimental import pallas as pl
from jax.experimental.pallas import tpu as pltpu

EPS = 1e-6
N_MOD = 6
A_GROUPS = 8
A_CHUNK = 128
B_HEADS = 4
B_DK = 128
B_DV = 256
QK_W = B_HEADS * B_DK
GATE_RANK = 16
GATE_TEMP = 16.0
GLA_CHUNK = 64
GLA_GROUP = 4 * GLA_CHUNK
LANES = 128
V7X_VMEM_BYTES = 64 * 1024 * 1024

PROJ_TM = 512
PROJ_NC = 256
TAIL_TM = 256
FF_CHUNK = 256

F32 = jnp.float32
BF16 = jnp.bfloat16


def _dot(a, b):
    return jnp.dot(a, b, preferred_element_type=F32)


def _resident(shape):
    nd = len(shape)
    return pl.BlockSpec(shape, lambda *_: (0,) * nd, pipeline_mode=pl.Buffered(1))


def _rmsnorm_f32(x, g):
    return x * lax.rsqrt(jnp.mean(x * x, axis=-1, keepdims=True) + EPS) * g


def _ada_kernel(c_ref, w_ref, b_ref, o_ref):
    c = c_ref[...]
    s = (c * jax.nn.sigmoid(c)).astype(BF16)
    o_ref[...] = _dot(s, w_ref[...].astype(BF16)) + b_ref[...]


def _ada(cc, w_ada, b_ada):
    rows, d = cc.shape
    n = w_ada.shape[1]
    bn = 512
    return pl.pallas_call(
        _ada_kernel,
        grid=(n // bn,),
        in_specs=[
            pl.BlockSpec((rows, d), lambda j: (0, 0)),
            pl.BlockSpec((d, bn), lambda j: (0, j)),
            pl.BlockSpec((1, bn), lambda j: (0, j)),
        ],
        out_specs=pl.BlockSpec((rows, bn), lambda j: (0, j)),
        out_shape=jax.ShapeDtypeStruct((rows, n), F32),
        name="ada",
    )(cc, w_ada, b_ada.reshape(1, n))


def _gelu(x):
    return 0.5 * x * (1.0 + lax.erf(x * (2.0 ** -0.5)))


def _log_sigmoid(z):
    return jnp.minimum(z, 0.0) - jnp.log(1.0 + jnp.exp(-jnp.abs(z)))


def _proj_kernel(latent, d, x_ref, mod_ref, n1g_ref, w_ref, wab_ref, bab_ref, tri_ref,
                 lng_ref, lnb_ref, ws_ref, bsp_ref, wba_ref, *outs):
    if latent:
        (ya_ref, qf_ref, kf_ref, qb_ref, kb_ref, vv_ref, r_ref, gb_ref, af_ref, ab_ref,
         qk_scr, z_scr, b_scr, hb_scr, a_scr, ug_scr, vg_scr, vn_scr, sga_scr) = outs
    else:
        kf_ref, kb_ref, vv_ref, af_ref, ab_ref, qk_scr, z_scr, b_scr, hb_scr = outs
    tm = x_ref.shape[0]
    n_ch = tm // GLA_CHUNK
    cpg = GLA_GROUP // GLA_CHUNK
    n_nc = d // PROJ_NC
    n_sp = tm // A_CHUNK
    gd = d // A_GROUPS
    n_qk = 2 * QK_W // PROJ_NC
    o_u, o_v, o_q, o_vv, o_r = 0, d, 2 * d, 2 * d + 2 * QK_W, 3 * d + 2 * QK_W
    o_ga, o_gb, o_al = o_r + d, o_r + 2 * d, o_r + 3 * d

    def chunk(j):
        return slice(j * PROJ_NC, (j + 1) * PROJ_NC)

    def proj(off, width=PROJ_NC, j=0):
        return _dot(hb_scr[...], w_ref[:, off + j * width: off + (j + 1) * width])

    def do_hb():
        xn = _rmsnorm_f32(x_ref[...], n1g_ref[...])
        hb_scr[...] = (xn * (1.0 + mod_ref[1:2, :]) + mod_ref[0:1, :]).astype(BF16)

    def do_vv(j):
        vv_ref[:, chunk(j)] = proj(o_vv, j=j).astype(BF16)

    def do_qk(j):
        qk = proj(o_q, j=j)
        qk_scr[:, chunk(j)] = qk * (B_DK ** -0.5) if (j + 1) * PROJ_NC <= QK_W else qk

    def do_z():
        al = proj(o_al, LANES).astype(BF16)
        z_scr[...] = _dot(al, wab_ref[...]) + bab_ref[...]

    def do_v(j):
        vg_scr[:, chunk(j)] = _gelu(proj(o_v, j=j))

    def do_u_ga(j):
        ug_scr[:, chunk(j)] = _gelu(proj(o_u, j=j)).astype(BF16)
        sga_scr[:, chunk(j)] = jax.nn.sigmoid(proj(o_ga, j=j)).astype(BF16)

    def do_r_gb(j):
        r_ref[:, chunk(j)] = proj(o_r, j=j).astype(BF16)
        gb_ref[:, chunk(j)] = proj(o_gb, j=j).astype(BF16)

    def layer_norm_rows(c):
        rows = slice(c * A_CHUNK, (c + 1) * A_CHUNK)
        vg = vg_scr[rows, :]
        vc = vg - jnp.mean(vg, axis=-1, keepdims=True)
        vn_scr[rows, :] = (vc * lax.rsqrt(jnp.mean(vc * vc, axis=-1, keepdims=True) + EPS)
                           * lng_ref[...] + lnb_ref[...]).astype(BF16)

    def spatial_group(gi):
        gcols = slice(gi * gd, (gi + 1) * gd)
        rhs = jnp.concatenate([vn_scr[c * A_CHUNK:(c + 1) * A_CHUNK, gcols] for c in range(n_sp)], axis=1)
        mixed = _dot(ws_ref[gi], rhs)
        for c in range(n_sp):
            rows = slice(c * A_CHUNK, (c + 1) * A_CHUNK)
            a_scr[rows, gcols] = (ug_scr[rows, gcols].astype(F32)
                                  * (mixed[:, c * gd:(c + 1) * gd] + bsp_ref[:, gcols])).astype(BF16)

    def do_ya():
        ya_ref[...] = (sga_scr[...].astype(F32) * _dot(a_scr[...], wba_ref[...])).astype(BF16)

    def decay(rev):
        k_ref, a_ref = (kb_ref, ab_ref) if rev else (kf_ref, af_ref)
        zc = slice(QK_W if rev else 0, 2 * QK_W if rev else QK_W)
        g = _log_sigmoid(z_scr[:, zc]) / GATE_TEMP
        g_hi = g.astype(BF16)
        g_lo = (g - g_hi.astype(F32)).astype(BF16)
        tri = tri_ref[1 if rev else 0]
        for gi in range(tm // GLA_GROUP):
            rows = slice(gi * GLA_GROUP, (gi + 1) * GLA_GROUP)
            b_scr[rows, :] = _dot(tri, g_hi[rows]) + _dot(tri, g_lo[rows])
        for h in range(B_HEADS):
            cols = slice(h * B_DK, (h + 1) * B_DK)
            b = b_scr[:, cols]
            k_ref[cols, :] = jnp.transpose(qk_scr[:, QK_W + h * B_DK: QK_W + (h + 1) * B_DK]
                                           * jnp.exp(-b)).astype(BF16)
            if latent:
                q_ref = qb_ref if rev else qf_ref
                q_ref[:, cols] = (qk_scr[:, cols] * jnp.exp(b)).astype(BF16)
            tot = jnp.sum(g[:, cols].reshape(n_ch, GLA_CHUNK, B_DK), axis=1)
            tot = jnp.concatenate([tot, jnp.zeros((LANES - n_ch, B_DK), F32)], axis=0)
            a_t = jnp.exp(jnp.transpose(tot))
            for gi in range(tm // GLA_GROUP):
                a_ref[gi, cols, :] = a_t[:, gi * cpg:(gi + 1) * cpg]

    do_hb()
    for j in range(n_nc):
        do_vv(j)
    for j in range(0 if latent else n_qk // 2, n_qk):
        do_qk(j)
    do_z()
    if latent:
        for j in range(n_nc):
            do_v(j)
        for j in range(n_nc):
            do_u_ga(j)
        for c in range(n_sp):
            layer_norm_rows(c)
        for j in range(n_nc):
            do_r_gb(j)
        for gi in range(A_GROUPS):
            spatial_group(gi)
        do_ya()
    for rev in (False, True):
        decay(rev)


def _proj(latent, x, mod, n1g, w_all, wab, bab, tri, lng, lnb, ws, bsp, wba, tm):
    bsz, t, d = x.shape
    nt = t // tm
    gpt = tm // GLA_GROUP
    cpg = GLA_GROUP // GLA_CHUNK
    tok = lambda w: pl.BlockSpec((None, tm, w), lambda b, i: (b, i, 0))
    ktr = pl.BlockSpec((None, QK_W, tm), lambda b, i: (b, 0, i))
    dec = pl.BlockSpec((None, gpt, QK_W, cpg), lambda b, i: (b, i, 0, 0))
    sd = lambda w, dt: jax.ShapeDtypeStruct((bsz, t, w), dt)
    sktr = jax.ShapeDtypeStruct((bsz, QK_W, t), BF16)
    sdec = jax.ShapeDtypeStruct((bsz, t // GLA_GROUP, QK_W, cpg), F32)
    scratch = [pltpu.VMEM((tm, 2 * QK_W), F32), pltpu.VMEM((tm, 2 * QK_W), F32),
               pltpu.VMEM((tm, QK_W), F32), pltpu.VMEM((tm, d), BF16)]
    if latent:
        out_specs = [tok(d), tok(QK_W), ktr, tok(QK_W), ktr, tok(d), tok(d), tok(d), dec, dec]
        out_shape = [sd(d, BF16), sd(QK_W, BF16), sktr, sd(QK_W, BF16), sktr,
                     sd(d, BF16), sd(d, BF16), sd(d, BF16), sdec, sdec]
        scratch += [pltpu.VMEM((tm, d), BF16), pltpu.VMEM((tm, d), BF16), pltpu.VMEM((tm, d), F32),
                    pltpu.VMEM((tm, d), BF16), pltpu.VMEM((tm, d), BF16)]
    else:
        out_specs = [ktr, ktr, tok(d), dec, dec]
        out_shape = [sktr, sktr, sd(d, BF16), sdec, sdec]
    return pl.pallas_call(
        functools.partial(_proj_kernel, latent, d),
        grid=(bsz, nt),
        in_specs=[
            tok(d),
            pl.BlockSpec((None, 8, d), lambda b, i: (b, 0, 0)),
            _resident(n1g.shape), _resident(w_all.shape), _resident(wab.shape), _resident(bab.shape),
            _resident(tri.shape), _resident(lng.shape), _resident(lnb.shape), _resident(ws.shape),
            _resident(bsp.shape), _resident(wba.shape),
        ],
        out_specs=out_specs,
        out_shape=out_shape,
        scratch_shapes=scratch,
        compiler_params=pltpu.CompilerParams(
            dimension_semantics=("parallel", "parallel"),
            vmem_limit_bytes=V7X_VMEM_BYTES * 7 // 8),
        name="proj_lat" if latent else "proj_ctx",
    )(x, mod, n1g, w_all, wab, bab, tri, lng, lnb, ws, bsp, wba)


def _gla_kernel(t, tc, qf_ref, ktf_ref, qb_ref, ktb_ref, v_ref, af_ref, ab_ref,
                ktcf_ref, ktcb_ref, vc_ref, acf_ref, acb_ref, gain_ref, o_ref,
                part_ref, att_scr, oin_scr, x_scr):
    c = GLA_CHUNK
    gsz = GLA_GROUP
    cpg = gsz // c
    n_g = t // gsz
    r = lax.broadcasted_iota(jnp.int32, (gsz, gsz), 0)
    col = lax.broadcasted_iota(jnp.int32, (gsz, gsz), 1)
    blk = r - r % c

    def bcast(a_col):
        return jnp.broadcast_to(a_col, (B_DK, B_DV))

    def chunk_rows(v, n):
        parts = []
        if n:
            parts.append(jnp.zeros((n * c, v.shape[1]), v.dtype))
        parts.append(v[n * c:(n + 1) * c])
        if n < cpg - 1:
            parts.append(jnp.zeros(((cpg - 1 - n) * c, v.shape[1]), v.dtype))
        return jnp.concatenate(parts, axis=0)

    class Direction:
        def __init__(self, idx, q_ref, kt_ref, a_ref, reverse):
            self.idx, self.q_ref, self.kt_ref, self.a_ref, self.reverse = idx, q_ref, kt_ref, a_ref, reverse
            self.order = range(cpg - 1, -1, -1) if reverse else range(cpg)

        def group(self, p):
            return (n_g - 1 - p) if self.reverse else p

        def rows(self, p):
            return pl.ds(pl.multiple_of(self.group(p) * gsz, gsz), gsz)

        def scores(self, p, slot):
            rows = self.rows(p)
            att = _dot(self.q_ref[rows, :], self.kt_ref[:, rows])
            if self.reverse:
                att = jnp.where(col >= r, jnp.where(col < blk + c, att, 0.0), 0.0)
            else:
                att = jnp.where(col <= r, jnp.where(col >= blk, att, 0.0), 0.0)
            att_scr[self.idx, slot] = att.astype(BF16)

        def values(self, p, slot):
            rows = self.rows(p)
            v = v_ref[rows, :]
            kt = self.kt_ref[:, rows]
            oin_scr[self.idx, slot] = _dot(att_scr[self.idx, slot], v)
            for n in range(cpg):
                x_scr[self.idx, slot, n] = _dot(kt, chunk_rows(v, n))

        def state(self, p, slot, s, finish):
            rows = self.rows(p)
            g = self.group(p)
            q = self.q_ref[rows, :]
            o_inter = [None] * cpg
            for n in self.order:
                o_inter[n] = _dot(q[n * c:(n + 1) * c], s.astype(BF16))
                s = bcast(self.a_ref[g, :, n:n + 1]) * (s + x_scr[self.idx, slot, n])
            o = oin_scr[self.idx, slot] + jnp.concatenate(o_inter, axis=0)
            if finish:
                tot = o + part_ref[rows, :]
                y = tot * lax.rsqrt(jnp.mean(tot * tot, axis=-1, keepdims=True) + EPS) * gain_ref[...]
                o_ref[rows, :] = y.astype(o_ref.dtype)
            else:
                part_ref[rows, :] = o
            return s

    dirs = (Direction(0, qf_ref, ktf_ref, af_ref, False), Direction(1, qb_ref, ktb_ref, ab_ref, True))

    states = []
    ncg = tc // gsz
    for d, ktc_ref, ac_ref in zip(dirs, (ktcf_ref, ktcb_ref), (acf_ref, acb_ref)):
        s = jnp.zeros((B_DK, B_DV), F32)
        for gi in (range(ncg - 1, -1, -1) if d.reverse else range(ncg)):
            rows = slice(gi * gsz, (gi + 1) * gsz)
            kt = ktc_ref[:, rows]
            v = vc_ref[rows, :]
            for n in d.order:
                s = bcast(ac_ref[gi, :, n:n + 1]) * (s + _dot(kt, chunk_rows(v, n)))
        states.append(s)

    for d in dirs:
        d.scores(0, 0)
        d.scores(1, 1)
    for d in dirs:
        d.values(0, 0)

    def make_body(finish, base):
        def body(i, states):
            for u in range(2):
                p = base + 2 * i + u
                for d in dirs:
                    d.scores(p + 2, u)
                for d in dirs:
                    d.values(p + 1, 1 - u)
                states = tuple(d.state(p, u, s, finish) for d, s in zip(dirs, states))
            return states
        return body

    half = n_g // 2
    states = lax.fori_loop(0, half // 2, make_body(False, 0), tuple(states))
    states = lax.fori_loop(0, (half - 2) // 2, make_body(True, half), states)

    for d in dirs:
        d.values(n_g - 1, 1)
    states = [d.state(n_g - 2, 0, s, True) for d, s in zip(dirs, states)]
    for d, s in zip(dirs, states):
        d.state(n_g - 1, 1, s, True)


def _gla(qf, ktf, qb, ktb, vv, af, ab, ktcf, ktcb, vc, acf, acb, gain):
    bsz, t, d = vv.shape
    tc = vc.shape[1]
    seq = lambda n, w: pl.BlockSpec((None, n, w), lambda b, h: (b, 0, h))
    seqt = lambda n: pl.BlockSpec((None, B_DK, n), lambda b, h: (b, h, 0))
    dec = lambda a: pl.BlockSpec((None, a.shape[1], B_DK, a.shape[3]), lambda b, h: (b, 0, h, 0))
    return pl.pallas_call(
        functools.partial(_gla_kernel, t, tc),
        grid=(bsz, B_HEADS),
        in_specs=[seq(t, B_DK), seqt(t), seq(t, B_DK), seqt(t), seq(t, B_DV),
                  dec(af), dec(ab), seqt(tc), seqt(tc), seq(tc, B_DV), dec(acf), dec(acb),
                  pl.BlockSpec((1, B_DV), lambda b, h: (0, h))],
        out_specs=seq(t, B_DV),
        out_shape=jax.ShapeDtypeStruct((bsz, t, d), BF16),
        scratch_shapes=[pltpu.VMEM((t, B_DV), F32),
                        pltpu.VMEM((2, 2, GLA_GROUP, GLA_GROUP), BF16),
                        pltpu.VMEM((2, 2, GLA_GROUP, B_DV), F32),
                        pltpu.VMEM((2, 2, GLA_GROUP // GLA_CHUNK, B_DK, B_DV), F32)],
        compiler_params=pltpu.CompilerParams(
            dimension_semantics=("parallel", "parallel"),
            vmem_limit_bytes=V7X_VMEM_BYTES * 3 // 4),
        name="gla",
    )(qf, ktf, qb, ktb, vv, af, ab, ktcf, ktcb, vc, acf, acb, gain)


def _tail_kernel(d_ff, x_ref, ya_ref, bl_ref, r_ref, gb_ref, mod_ref, wbb_ref, wo_ref, n2g_ref,
                 wfi_ref, wfo_ref, fng_ref, o_ref, f_scr):
    r = r_ref[...].astype(F32)
    b_out = (bl_ref[...].astype(F32) * (r * jax.nn.sigmoid(r))).astype(BF16)
    y = ya_ref[...].astype(F32) + jax.nn.sigmoid(gb_ref[...].astype(F32)) * _dot(b_out, wbb_ref[...])
    mix = _dot(y.astype(BF16), wo_ref[...])
    x1 = x_ref[...] + mod_ref[2:3, :] * mix
    h2 = (_rmsnorm_f32(x1, n2g_ref[...]) * (1.0 + mod_ref[4:5, :]) + mod_ref[3:4, :]).astype(BF16)
    for j in range(d_ff // FF_CHUNK):
        cols = slice(j * FF_CHUNK, (j + 1) * FF_CHUNK)
        a = _dot(h2, wfi_ref[:, cols])
        g = _dot(h2, wfi_ref[:, d_ff + j * FF_CHUNK: d_ff + (j + 1) * FF_CHUNK])
        f_scr[:, cols] = (g * jax.nn.sigmoid(g) * a).astype(BF16)
    x2 = x1 + mod_ref[5:6, :] * _dot(f_scr[...], wfo_ref[...])
    o_ref[...] = _rmsnorm_f32(x2, fng_ref[...])


def _tail(x, ya, bl, r, gb, mod, wbb, wo, n2g, wfi, wfo, fng, tm):
    bsz, t, d = x.shape
    d_ff = wfo.shape[0]
    tok = pl.BlockSpec((None, tm, d), lambda b, i: (b, i, 0))
    return pl.pallas_call(
        functools.partial(_tail_kernel, d_ff),
        grid=(bsz, t // tm),
        in_specs=[tok, tok, tok, tok, tok, pl.BlockSpec((None, 8, d), lambda b, i: (b, 0, 0)),
                  _resident(wbb.shape), _resident(wo.shape), _resident(n2g.shape),
                  _resident(wfi.shape), _resident(wfo.shape), _resident(fng.shape)],
        out_specs=tok,
        out_shape=jax.ShapeDtypeStruct((bsz, t, d), x.dtype),
        scratch_shapes=[pltpu.VMEM((tm, d_ff), BF16)],
        compiler_params=pltpu.CompilerParams(
            dimension_semantics=("parallel", "parallel"),
            vmem_limit_bytes=V7X_VMEM_BYTES * 7 // 8),
        name="tail",
    )(x, ya, bl, r, gb, mod, wbb, wo, n2g, wfi, wfo, fng)


def kernel(x, c, ctx, c_ctx, w_ada, b_ada, norm1_g, w_in, ln_v_g, ln_v_b, w_spatial, b_spatial,
           w_alpha_f, b_alpha_f, w_alpha_b, b_alpha_b, gla_norm_g, w_branch_a, w_branch_b,
           w_out, norm2_g, w_ffn_in, w_ffn_out, final_norm_g):
    bsz, t, d = x.shape
    tc = ctx.shape[1]
    depth = w_ada.shape[0]
    assert depth == 1 and d == B_HEADS * B_DV and d % A_GROUPS == 0
    assert t % (4 * GLA_GROUP) == 0 and t % PROJ_TM == 0 and PROJ_TM % GLA_GROUP == 0 and t % TAIL_TM == 0
    assert tc % A_CHUNK == 0 and tc % GLA_GROUP == 0
    l = 0

    rows = -(-(bsz + 1) // 8) * 8
    cc = jnp.zeros((rows, d), F32).at[:bsz].set(c).at[bsz].set(c_ctx)
    mod_all = _ada(cc, w_ada[l], b_ada[l]).reshape(rows, N_MOD, d)
    mod_all = jnp.pad(mod_all, ((0, 0), (0, 8 - N_MOD), (0, 0)))
    mod_lat = mod_all[:bsz]
    mod_ctx = jnp.broadcast_to(mod_all[bsz:bsz + 1], (bsz, 8, d))

    w = w_in[l]
    o_al = 3 * d + 2 * QK_W + d
    o_ga = o_al + 2 * GATE_RANK
    w_all = jnp.concatenate(
        [w[:, :o_al], w[:, o_ga:], jnp.pad(w[:, o_al:o_ga], ((0, 0), (0, LANES - 2 * GATE_RANK)))],
        axis=1).astype(BF16)
    wab = jnp.zeros((LANES, 2 * QK_W), F32)
    wab = wab.at[:GATE_RANK, :QK_W].set(w_alpha_f[l]).at[GATE_RANK:2 * GATE_RANK, QK_W:].set(w_alpha_b[l])
    wab = wab.astype(BF16)
    bab = jnp.concatenate([b_alpha_f[l], b_alpha_b[l]]).reshape(1, 2 * QK_W)
    n1g = norm1_g[l].reshape(1, d)
    lng = ln_v_g[l].reshape(1, d)
    lnb = ln_v_b[l].reshape(1, d)
    ws = w_spatial[l].astype(BF16)
    bsp = jnp.repeat(jnp.transpose(b_spatial[l]), d // A_GROUPS, axis=1)
    wba = w_branch_a[l].astype(BF16)

    pos = jnp.arange(GLA_GROUP)
    same = (pos[:, None] // GLA_CHUNK) == (pos[None, :] // GLA_CHUNK)
    tri = jnp.stack([same & (pos[None, :] <= pos[:, None]), same & (pos[None, :] >= pos[:, None])]).astype(BF16)

    common = (n1g, w_all, wab, bab, tri, lng, lnb, ws, bsp, wba)
    ya, qf, ktf, qb, ktb, vv, r, gb, af, ab = _proj(True, x, mod_lat, *common, tm=PROJ_TM)
    ktcf, ktcb, vc, acf, acb = _proj(False, ctx, mod_ctx, *common, tm=tc)
    bl = _gla(qf, ktf, qb, ktb, vv, af, ab, ktcf, ktcb, vc, acf, acb,
              gla_norm_g[l].reshape(1, d))

    return _tail(x, ya, bl, r, gb, mod_lat, w_branch_b[l].astype(BF16), w_out[l].astype(BF16),
                 norm2_g[l].reshape(1, d), w_ffn_in[l].astype(BF16), w_ffn_out[l].astype(BF16),
                 final_norm_g.reshape(1, d), tm=TAIL_TM)
```

```python
import functools

import jax
import jax.numpy as jnp
from jax import lax
from jax.experimental import pallas as pl
from jax.experimental.pallas import tpu as pltpu

EPS = 1e-6
N_MOD = 6
A_GROUPS = 8
A_CHUNK = 128
B_HEADS = 4
B_DK = 128
B_DV = 256
QK_W = B_HEADS * B_DK
GATE_RANK = 16
GATE_TEMP = 16.0
GLA_CHUNK = 64
GLA_GROUP = 4 * GLA_CHUNK
LANES = 128
V7X_VMEM_BYTES = 64 * 1024 * 1024

PROJ_TM = 512
PROJ_NC = 256
TAIL_TM = 512
FF_CHUNK = 256

F32 = jnp.float32
BF16 = jnp.bfloat16


def _dot(a, b):
    return jnp.dot(a, b, preferred_element_type=F32)


def _resident(shape):
    nd = len(shape)
    return pl.BlockSpec(shape, lambda *_: (0,) * nd, pipeline_mode=pl.Buffered(1))


def _rmsnorm_f32(x, g):
    return x * lax.rsqrt(jnp.mean(x * x, axis=-1, keepdims=True) + EPS) * g


def _ada_kernel(c_ref, w_ref, b_ref, o_ref):
    c = c_ref[...]
    s = (c * jax.nn.sigmoid(c)).astype(BF16)
    o_ref[...] = _dot(s, w_ref[...].astype(BF16)) + b_ref[...]


def _ada(cc, w_ada, b_ada):
    rows, d = cc.shape
    n = w_ada.shape[1]
    bn = 512
    return pl.pallas_call(
        _ada_kernel,
        grid=(n // bn,),
        in_specs=[
            pl.BlockSpec((rows, d), lambda j: (0, 0)),
            pl.BlockSpec((d, bn), lambda j: (0, j)),
            pl.BlockSpec((1, bn), lambda j: (0, j)),
        ],
        out_specs=pl.BlockSpec((rows, bn), lambda j: (0, j)),
        out_shape=jax.ShapeDtypeStruct((rows, n), F32),
        name="ada",
    )(cc, w_ada, b_ada.reshape(1, n))


def _gelu(x):
    return 0.5 * x * (1.0 + lax.erf(x * (2.0 ** -0.5)))


def _log_sigmoid(z):
    return jnp.minimum(z, 0.0) - jnp.log(1.0 + jnp.exp(-jnp.abs(z)))


def _tie(dst_ref, src_ref):
    bits = pltpu.bitcast(src_ref[...].astype(F32), jnp.uint32)
    zero = pltpu.bitcast((bits >> 16) >> 16, F32)
    dst_ref[...] = dst_ref[...] + zero.astype(dst_ref.dtype)


def _proj_kernel(latent, d, x_ref, mod_ref, n1g_ref, w_ref, wg_ref, wal_ref, wab_ref, bab_ref, tri_ref,
                 lng_ref, lnb_ref, ws_ref, bsp_ref, wba_ref, *outs):
    if latent:
        (ya_ref, qf_ref, kf_ref, qb_ref, kb_ref, vv_ref, r_ref, gb_ref, af_ref, ab_ref,
         qk_scr, z_scr, b_scr, hb_scr, a_scr, ug_scr, vg_scr, vn_scr, sga_scr) = outs
    else:
        kf_ref, kb_ref, vv_ref, af_ref, ab_ref, qk_scr, z_scr, b_scr, hb_scr = outs
    tm = x_ref.shape[0]
    n_ch = tm // GLA_CHUNK
    cpg = GLA_GROUP // GLA_CHUNK
    n_nc = d // PROJ_NC
    n_sp = tm // A_CHUNK
    gd = d // A_GROUPS
    n_qk = 2 * QK_W // PROJ_NC
    o_u, o_v, o_q, o_vv, o_r = 0, d, 2 * d, 2 * d + 2 * QK_W, 3 * d + 2 * QK_W
    o_ga, o_gb = 0, d

    def chunk(j):
        return slice(j * PROJ_NC, (j + 1) * PROJ_NC)

    def proj(off, j, wt_ref=w_ref):
        return _dot(hb_scr[...], wt_ref[:, off + j * PROJ_NC: off + (j + 1) * PROJ_NC])

    def do_hb():
        xn = _rmsnorm_f32(x_ref[...], n1g_ref[...])
        hb_scr[...] = (xn * (1.0 + mod_ref[1:2, :]) + mod_ref[0:1, :]).astype(BF16)

    def do_vv(j):
        vv_ref[:, chunk(j)] = proj(o_vv, j).astype(BF16)

    def do_qk(j):
        qk = proj(o_q, j)
        qk_scr[:, chunk(j)] = qk * (B_DK ** -0.5) if (j + 1) * PROJ_NC <= QK_W else qk

    def do_z():
        al = _dot(hb_scr[...], wal_ref[...]).astype(BF16)
        z_scr[...] = _dot(al, wab_ref[...]) + bab_ref[...]

    def do_v(j):
        vg_scr[:, chunk(j)] = _gelu(proj(o_v, j))

    def do_u_ga(j):
        ug_scr[:, chunk(j)] = _gelu(proj(o_u, j)).astype(BF16)
        sga_scr[:, chunk(j)] = jax.nn.sigmoid(proj(o_ga, j, wg_ref)).astype(BF16)

    def do_r_gb(j):
        r_ref[:, chunk(j)] = proj(o_r, j).astype(BF16)
        gb_ref[:, chunk(j)] = proj(o_gb, j, wg_ref).astype(BF16)

    def layer_norm_rows(c):
        rows = slice(c * A_CHUNK, (c + 1) * A_CHUNK)
        vg = vg_scr[rows, :]
        vc = vg - jnp.mean(vg, axis=-1, keepdims=True)
        vn_scr[rows, :] = (vc * lax.rsqrt(jnp.mean(vc * vc, axis=-1, keepdims=True) + EPS)
                           * lng_ref[...] + lnb_ref[...]).astype(BF16)

    def spatial_group(gi):
        gcols = slice(gi * gd, (gi + 1) * gd)
        rhs = jnp.concatenate([vn_scr[c * A_CHUNK:(c + 1) * A_CHUNK, gcols] for c in range(n_sp)], axis=1)
        mixed = _dot(ws_ref[gi], rhs)
        for c in range(n_sp):
            rows = slice(c * A_CHUNK, (c + 1) * A_CHUNK)
            a_scr[rows, gcols] = (ug_scr[rows, gcols].astype(F32)
                                  * (mixed[:, c * gd:(c + 1) * gd] + bsp_ref[:, gcols])).astype(BF16)

    def do_ya():
        ya_ref[...] = (sga_scr[...].astype(F32) * _dot(a_scr[...], wba_ref[...])).astype(BF16)

    def decay(rev):
        k_ref, a_ref = (kb_ref, ab_ref) if rev else (kf_ref, af_ref)
        zc = slice(QK_W if rev else 0, 2 * QK_W if rev else QK_W)
        g = _log_sigmoid(z_scr[:, zc]) / GATE_TEMP
        g_hi = g.astype(BF16)
        g_lo = (g - g_hi.astype(F32)).astype(BF16)
        tri = tri_ref[1 if rev else 0]
        for gi in range(tm // GLA_GROUP):
            rows = slice(gi * GLA_GROUP, (gi + 1) * GLA_GROUP)
            b_scr[rows, :] = _dot(tri, g_hi[rows]) + _dot(tri, g_lo[rows])
        for h in range(B_HEADS):
            cols = slice(h * B_DK, (h + 1) * B_DK)
            b = b_scr[:, cols]
            k_ref[cols, :] = jnp.transpose(qk_scr[:, QK_W + h * B_DK: QK_W + (h + 1) * B_DK]
                                           * jnp.exp(-b)).astype(BF16)
            if latent:
                q_ref = qb_ref if rev else qf_ref
                q_ref[:, cols] = (qk_scr[:, cols] * jnp.exp(b)).astype(BF16)
            tot = jnp.sum(g[:, cols].reshape(n_ch, GLA_CHUNK, B_DK), axis=1)
            tot = jnp.concatenate([tot, jnp.zeros((LANES - n_ch, B_DK), F32)], axis=0)
            a_t = jnp.exp(jnp.transpose(tot))
            for gi in range(tm // GLA_GROUP):
                a_ref[gi, cols, :] = a_t[:, gi * cpg:(gi + 1) * cpg]

    do_hb()
    for j in range(n_nc // 2 if latent else n_nc):
        do_vv(j)
    for j in range(0 if latent else n_qk // 2, n_qk):
        do_qk(j)
    do_z()
    if latent:
        for j in range(n_nc):
            do_v(j)
        for j in range(n_nc):
            do_u_ga(j)
        for c in range(n_sp):
            layer_norm_rows(c)
        for j in range(n_nc):
            do_r_gb(j)
        for gi in range(A_GROUPS):
            spatial_group(gi)
        do_ya()
    for rev in (False, True):
        decay(rev)
    if latent:
        _tie(hb_scr.at[0:16, 0:LANES], kb_ref.at[QK_W - 16:QK_W, 0:LANES])
        for j in range(n_nc // 2, n_nc):
            do_vv(j)


def _proj(latent, x, mod, n1g, w_main, w_gate, w_al, wab, bab, tri, lng, lnb, ws, bsp, wba, tm):
    bsz, t, d = x.shape
    nt = t // tm
    gpt = tm // GLA_GROUP
    cpg = GLA_GROUP // GLA_CHUNK
    tok = lambda w: pl.BlockSpec((None, tm, w), lambda b, i: (b, i, 0))
    ktr = pl.BlockSpec((None, QK_W, tm), lambda b, i: (b, 0, i))
    dec = pl.BlockSpec((None, gpt, QK_W, cpg), lambda b, i: (b, i, 0, 0))
    sd = lambda w, dt: jax.ShapeDtypeStruct((bsz, t, w), dt)
    sktr = jax.ShapeDtypeStruct((bsz, QK_W, t), BF16)
    sdec = jax.ShapeDtypeStruct((bsz, t // GLA_GROUP, QK_W, cpg), F32)
    scratch = [pltpu.VMEM((tm, 2 * QK_W), F32), pltpu.VMEM((tm, 2 * QK_W), F32),
               pltpu.VMEM((tm, QK_W), F32), pltpu.VMEM((tm, d), BF16)]
    if latent:
        out_specs = [tok(d), tok(QK_W), ktr, tok(QK_W), ktr, tok(d), tok(d), tok(d), dec, dec]
        out_shape = [sd(d, BF16), sd(QK_W, BF16), sktr, sd(QK_W, BF16), sktr,
                     sd(d, BF16), sd(d, BF16), sd(d, BF16), sdec, sdec]
        scratch += [pltpu.VMEM((tm, d), BF16), pltpu.VMEM((tm, d), BF16), pltpu.VMEM((tm, d), F32),
                    pltpu.VMEM((tm, d), BF16), pltpu.VMEM((tm, d), BF16)]
    else:
        out_specs = [ktr, ktr, tok(d), dec, dec]
        out_shape = [sktr, sktr, sd(d, BF16), sdec, sdec]
    return pl.pallas_call(
        functools.partial(_proj_kernel, latent, d),
        grid=(bsz, nt),
        in_specs=[
            tok(d),
            pl.BlockSpec((None, 8, d), lambda b, i: (b, 0, 0)),
            _resident(n1g.shape), _resident(w_main.shape), _resident(w_gate.shape), _resident(w_al.shape),
            _resident(wab.shape), _resident(bab.shape),
            _resident(tri.shape), _resident(lng.shape), _resident(lnb.shape), _resident(ws.shape),
            _resident(bsp.shape), _resident(wba.shape),
        ],
        out_specs=out_specs,
        out_shape=out_shape,
        scratch_shapes=scratch,
        compiler_params=pltpu.CompilerParams(
            dimension_semantics=("parallel", "parallel"),
            vmem_limit_bytes=V7X_VMEM_BYTES * 7 // 8),
        name="proj_lat" if latent else "proj_ctx",
    )(x, mod, n1g, w_main, w_gate, w_al, wab, bab, tri, lng, lnb, ws, bsp, wba)


def _gla_kernel(t, tc, qf_ref, ktf_ref, qb_ref, ktb_ref, v_ref, af_ref, ab_ref,
                ktcf_ref, ktcb_ref, vc_ref, acf_ref, acb_ref, gain_ref, o_ref,
                part_ref, att_scr, oin_scr, x_scr):
    c = GLA_CHUNK
    gsz = GLA_GROUP
    cpg = gsz // c
    n_g = t // gsz
    r = lax.broadcasted_iota(jnp.int32, (gsz, gsz), 0)
    col = lax.broadcasted_iota(jnp.int32, (gsz, gsz), 1)
    blk = r - r % c

    def bcast(a_col):
        return jnp.broadcast_to(a_col, (B_DK, B_DV))

    def chunk_rows(v, n):
        parts = []
        if n:
            parts.append(jnp.zeros((n * c, v.shape[1]), v.dtype))
        parts.append(v[n * c:(n + 1) * c])
        if n < cpg - 1:
            parts.append(jnp.zeros(((cpg - 1 - n) * c, v.shape[1]), v.dtype))
        return jnp.concatenate(parts, axis=0)

    class Direction:
        def __init__(self, idx, q_ref, kt_ref, a_ref, reverse):
            self.idx, self.q_ref, self.kt_ref, self.a_ref, self.reverse = idx, q_ref, kt_ref, a_ref, reverse
            self.order = range(cpg - 1, -1, -1) if reverse else range(cpg)

        def group(self, p):
            return (n_g - 1 - p) if self.reverse else p

        def rows(self, p):
            return slice(self.group(p) * gsz, (self.group(p) + 1) * gsz)

        def scores(self, p, slot):
            rows = self.rows(p)
            att = _dot(self.q_ref[rows, :], self.kt_ref[:, rows])
            if self.reverse:
                att = jnp.where(col >= r, jnp.where(col < blk + c, att, 0.0), 0.0)
            else:
                att = jnp.where(col <= r, jnp.where(col >= blk, att, 0.0), 0.0)
            att_scr[self.idx, slot] = att.astype(BF16)

        def values(self, p, slot):
            rows = self.rows(p)
            v = v_ref[rows, :]
            kt = self.kt_ref[:, rows]
            oin_scr[self.idx, slot] = _dot(att_scr[self.idx, slot], v)
            for n in range(cpg):
                x_scr[self.idx, slot, n] = _dot(kt, chunk_rows(v, n))

        def state(self, p, slot, s, finish):
            rows = self.rows(p)
            g = self.group(p)
            q = self.q_ref[rows, :]
            o_inter = [None] * cpg
            for n in self.order:
                o_inter[n] = _dot(q[n * c:(n + 1) * c], s.astype(BF16))
                s = bcast(self.a_ref[g, :, n:n + 1]) * (s + x_scr[self.idx, slot, n])
            o = oin_scr[self.idx, slot] + jnp.concatenate(o_inter, axis=0)
            if finish:
                tot = o + part_ref[rows, :]
                y = tot * lax.rsqrt(jnp.mean(tot * tot, axis=-1, keepdims=True) + EPS) * gain_ref[...]
                o_ref[rows, :] = y.astype(o_ref.dtype)
            else:
                part_ref[rows, :] = o
            return s

    dirs = (Direction(0, qf_ref, ktf_ref, af_ref, False), Direction(1, qb_ref, ktb_ref, ab_ref, True))

    states = []
    ncg = tc // gsz
    for d, ktc_ref, ac_ref in zip(dirs, (ktcf_ref, ktcb_ref), (acf_ref, acb_ref)):
        s = jnp.zeros((B_DK, B_DV), F32)
        for gi in (range(ncg - 1, -1, -1) if d.reverse else range(ncg)):
            rows = slice(gi * gsz, (gi + 1) * gsz)
            kt = ktc_ref[:, rows]
            v = vc_ref[rows, :]
            for n in d.order:
                s = bcast(ac_ref[gi, :, n:n + 1]) * (s + _dot(kt, chunk_rows(v, n)))
        states.append(s)

    for d in dirs:
        d.scores(0, 0)
        d.scores(1, 1)
    for d in dirs:
        d.values(0, 0)

    half = n_g // 2
    for p in range(n_g - 2):
        for d in dirs:
            d.scores(p + 2, p % 2)
        for d in dirs:
            d.values(p + 1, (p + 1) % 2)
        states = [d.state(p, p % 2, s, p >= half) for d, s in zip(dirs, states)]

    for d in dirs:
        d.values(n_g - 1, (n_g - 1) % 2)
    states = [d.state(n_g - 2, n_g % 2, s, True) for d, s in zip(dirs, states)]
    for d, s in zip(dirs, states):
        d.state(n_g - 1, (n_g - 1) % 2, s, True)


def _gla(qf, ktf, qb, ktb, vv, af, ab, ktcf, ktcb, vc, acf, acb, gain):
    bsz, t, d = vv.shape
    tc = vc.shape[1]
    seq = lambda n, w: pl.BlockSpec((None, n, w), lambda b, h: (b, 0, h))
    seqt = lambda n: pl.BlockSpec((None, B_DK, n), lambda b, h: (b, h, 0))
    dec = lambda a: pl.BlockSpec((None, a.shape[1], B_DK, a.shape[3]), lambda b, h: (b, 0, h, 0))
    return pl.pallas_call(
        functools.partial(_gla_kernel, t, tc),
        grid=(bsz, B_HEADS),
        in_specs=[seq(t, B_DK), seqt(t), seq(t, B_DK), seqt(t), seq(t, B_DV),
                  dec(af), dec(ab), seqt(tc), seqt(tc), seq(tc, B_DV), dec(acf), dec(acb),
                  pl.BlockSpec((1, B_DV), lambda b, h: (0, h))],
        out_specs=seq(t, B_DV),
        out_shape=jax.ShapeDtypeStruct((bsz, t, d), BF16),
        scratch_shapes=[pltpu.VMEM((t, B_DV), F32),
                        pltpu.VMEM((2, 2, GLA_GROUP, GLA_GROUP), BF16),
                        pltpu.VMEM((2, 2, GLA_GROUP, B_DV), F32),
                        pltpu.VMEM((2, 2, GLA_GROUP // GLA_CHUNK, B_DK, B_DV), F32)],
        compiler_params=pltpu.CompilerParams(
            dimension_semantics=("parallel", "parallel"),
            vmem_limit_bytes=V7X_VMEM_BYTES * 3 // 4),
        name="gla",
    )(qf, ktf, qb, ktb, vv, af, ab, ktcf, ktcb, vc, acf, acb, gain)


def _tail_kernel(d_ff, x_ref, ya_ref, bl_ref, r_ref, gb_ref, mod_ref, wbb_ref, wo_ref, n2g_ref,
                 wfi_ref, wfo_ref, fng_ref, o_ref, f_scr):
    r = r_ref[...].astype(F32)
    b_out = (bl_ref[...].astype(F32) * (r * jax.nn.sigmoid(r))).astype(BF16)
    y = ya_ref[...].astype(F32) + jax.nn.sigmoid(gb_ref[...].astype(F32)) * _dot(b_out, wbb_ref[...])
    mix = _dot(y.astype(BF16), wo_ref[...])
    x1 = x_ref[...] + mod_ref[2:3, :] * mix
    h2 = (_rmsnorm_f32(x1, n2g_ref[...]) * (1.0 + mod_ref[4:5, :]) + mod_ref[3:4, :]).astype(BF16)
    for j in range(d_ff // FF_CHUNK):
        cols = slice(j * FF_CHUNK, (j + 1) * FF_CHUNK)
        a = _dot(h2, wfi_ref[:, cols])
        g = _dot(h2, wfi_ref[:, d_ff + j * FF_CHUNK: d_ff + (j + 1) * FF_CHUNK])
        f_scr[:, cols] = (g * jax.nn.sigmoid(g) * a).astype(BF16)
    x2 = x1 + mod_ref[5:6, :] * _dot(f_scr[...], wfo_ref[...])
    o_ref[...] = _rmsnorm_f32(x2, fng_ref[...])


def _tail(x, ya, bl, r, gb, mod, wbb, wo, n2g, wfi, wfo, fng, tm):
    bsz, t, d = x.shape
    d_ff = wfo.shape[0]
    tok = pl.BlockSpec((None, tm, d), lambda b, i: (b, i, 0))
    return pl.pallas_call(
        functools.partial(_tail_kernel, d_ff),
        grid=(bsz, t // tm),
        in_specs=[tok, tok, tok, tok, tok, pl.BlockSpec((None, 8, d), lambda b, i: (b, 0, 0)),
                  _resident(wbb.shape), _resident(wo.shape), _resident(n2g.shape),
                  _resident(wfi.shape), _resident(wfo.shape), _resident(fng.shape)],
        out_specs=tok,
        out_shape=jax.ShapeDtypeStruct((bsz, t, d), x.dtype),
        scratch_shapes=[pltpu.VMEM((tm, d_ff), BF16)],
        compiler_params=pltpu.CompilerParams(
            dimension_semantics=("parallel", "parallel"),
            vmem_limit_bytes=V7X_VMEM_BYTES * 7 // 8),
        name="tail",
    )(x, ya, bl, r, gb, mod, wbb, wo, n2g, wfi, wfo, fng)


def kernel(x, c, ctx, c_ctx, w_ada, b_ada, norm1_g, w_in, ln_v_g, ln_v_b, w_spatial, b_spatial,
           w_alpha_f, b_alpha_f, w_alpha_b, b_alpha_b, gla_norm_g, w_branch_a, w_branch_b,
           w_out, norm2_g, w_ffn_in, w_ffn_out, final_norm_g):
    bsz, t, d = x.shape
    tc = ctx.shape[1]
    depth = w_ada.shape[0]
    assert depth == 1 and d == B_HEADS * B_DV and d % A_GROUPS == 0
    assert t % (4 * GLA_GROUP) == 0 and t % PROJ_TM == 0 and PROJ_TM % GLA_GROUP == 0 and t % TAIL_TM == 0
    assert tc % A_CHUNK == 0 and tc % GLA_GROUP == 0
    l = 0

    rows = -(-(bsz + 1) // 8) * 8
    cc = jnp.zeros((rows, d), F32).at[:bsz].set(c).at[bsz].set(c_ctx)
    mod_all = _ada(cc, w_ada[l], b_ada[l]).reshape(rows, N_MOD, d)
    mod_all = jnp.pad(mod_all, ((0, 0), (0, 8 - N_MOD), (0, 0)))
    mod_lat = mod_all[:bsz]
    mod_ctx = jnp.broadcast_to(mod_all[bsz:bsz + 1], (bsz, 8, d))

    w = w_in[l]
    o_al = 3 * d + 2 * QK_W + d
    o_ga = o_al + 2 * GATE_RANK
    w_main = w[:, :o_al].astype(BF16)
    w_gate = w[:, o_ga:].astype(BF16)
    w_al = jnp.pad(w[:, o_al:o_ga], ((0, 0), (0, LANES - 2 * GATE_RANK))).astype(BF16)
    wab = jnp.zeros((LANES, 2 * QK_W), F32)
    wab = wab.at[:GATE_RANK, :QK_W].set(w_alpha_f[l]).at[GATE_RANK:2 * GATE_RANK, QK_W:].set(w_alpha_b[l])
    wab = wab.astype(BF16)
    bab = jnp.concatenate([b_alpha_f[l], b_alpha_b[l]]).reshape(1, 2 * QK_W)
    n1g = norm1_g[l].reshape(1, d)
    lng = ln_v_g[l].reshape(1, d)
    lnb = ln_v_b[l].reshape(1, d)
    ws = w_spatial[l].astype(BF16)
    bsp = jnp.repeat(jnp.transpose(b_spatial[l]), d // A_GROUPS, axis=1)
    wba = w_branch_a[l].astype(BF16)

    pos = jnp.arange(GLA_GROUP)
    same = (pos[:, None] // GLA_CHUNK) == (pos[None, :] // GLA_CHUNK)
    tri = jnp.stack([same & (pos[None, :] <= pos[:, None]), same & (pos[None, :] >= pos[:, None])]).astype(BF16)

    common = (n1g, w_main, w_gate, w_al, wab, bab, tri, lng, lnb, ws, bsp, wba)
    ya, qf, ktf, qb, ktb, vv, r, gb, af, ab = _proj(True, x, mod_lat, *common, tm=PROJ_TM)
    ktcf, ktcb, vc, acf, acb = _proj(False, ctx, mod_ctx, *common, tm=tc)
    bl = _gla(qf, ktf, qb, ktb, vv, af, ab, ktcf, ktcb, vc, acf, acb,
              gla_norm_g[l].reshape(1, d))

    return _tail(x, ya, bl, r, gb, mod_lat, w_branch_b[l].astype(BF16), w_out[l].astype(BF16),
                 norm2_g[l].reshape(1, d), w_ffn_in[l].astype(BF16), w_ffn_out[l].astype(BF16),
                 final_norm_g.reshape(1, d), tm=TAIL_TM)
```

```python
import functools

import jax
import jax.numpy as jnp
from jax import lax
from jax.experimental import pallas as pl
from jax.experimental.pallas import tpu as pltpu

EPS = 1e-6
N_MOD = 6
A_GROUPS = 8
A_CHUNK = 128
B_HEADS = 4
B_DK = 128
B_DV = 256
QK_W = B_HEADS * B_DK
GATE_RANK = 16
GATE_TEMP = 16.0
GLA_CHUNK = 64
GLA_GROUP = 4 * GLA_CHUNK
LANES = 128
V7X_VMEM_BYTES = 64 * 1024 * 1024

PROJ_TM = 512
PROJ_NC = 256
TAIL_TM = 512
FF_CHUNK = 256

F32 = jnp.float32
BF16 = jnp.bfloat16


def _dot(a, b):
    return jnp.dot(a, b, preferred_element_type=F32)


def _resident(shape):
    nd = len(shape)
    return pl.BlockSpec(shape, lambda *_: (0,) * nd, pipeline_mode=pl.Buffered(1))


def _rmsnorm_f32(x, g):
    return x * lax.rsqrt(jnp.mean(x * x, axis=-1, keepdims=True) + EPS) * g


def _ada_kernel(c_ref, w_ref, b_ref, o_ref):
    c = c_ref[...]
    s = (c * jax.nn.sigmoid(c)).astype(BF16)
    o_ref[...] = _dot(s, w_ref[...].astype(BF16)) + b_ref[...]


def _ada(cc, w_ada, b_ada):
    rows, d = cc.shape
    n = w_ada.shape[1]
    bn = 1024
    return pl.pallas_call(
        _ada_kernel,
        grid=(n // bn,),
        in_specs=[
            pl.BlockSpec((rows, d), lambda j: (0, 0)),
            pl.BlockSpec((d, bn), lambda j: (0, j)),
            pl.BlockSpec((1, bn), lambda j: (0, j)),
        ],
        out_specs=pl.BlockSpec((rows, bn), lambda j: (0, j)),
        out_shape=jax.ShapeDtypeStruct((rows, n), F32),
        name="ada",
    )(cc, w_ada, b_ada.reshape(1, n))


def _gelu(x):
    return 0.5 * x * (1.0 + lax.erf(x * (2.0 ** -0.5)))


def _log_sigmoid(z):
    return jnp.minimum(z, 0.0) - jnp.log(1.0 + jnp.exp(-jnp.abs(z)))


def _tie(dst_ref, src_ref):
    bits = pltpu.bitcast(src_ref[...].astype(F32), jnp.uint32)
    zero = pltpu.bitcast((bits >> 16) >> 16, F32)
    dst_ref[...] = dst_ref[...] + zero.astype(dst_ref.dtype)


def _proj_kernel(latent, d, x_ref, mod_ref, n1g_ref, wt_ref, wab_ref, bab_ref, tri_ref,
                 lng_ref, lnb_ref, ws_ref, bsp_ref, wba_ref, *outs):
    if latent:
        (ya_ref, qf_ref, kf_ref, qb_ref, kb_ref, vv_ref, r_ref, gb_ref, af_ref, ab_ref,
         qk_scr, z_scr, b_scr, hb_scr, a_scr, ug_scr, vg_scr, vn_scr, sga_scr) = outs
    else:
        kf_ref, kb_ref, vv_ref, af_ref, ab_ref, qk_scr, z_scr, b_scr, hb_scr = outs
    tm = x_ref.shape[0]
    n_ch = tm // GLA_CHUNK
    cpg = GLA_GROUP // GLA_CHUNK
    n_nc = d // PROJ_NC
    n_sp = tm // A_CHUNK
    gd = d // A_GROUPS
    n_qk = 2 * QK_W // PROJ_NC
    o_u, o_v, o_q, o_vv, o_r = 0, d, 2 * d, 2 * d + 2 * QK_W, 3 * d + 2 * QK_W
    o_al = o_r + d
    o_ga, o_gb = o_al + 2 * GATE_RANK, o_al + 2 * GATE_RANK + d

    def chunk(j):
        return slice(j * PROJ_NC, (j + 1) * PROJ_NC)

    def proj(off, j, width=PROJ_NC):
        wt = wt_ref[off + j * width: off + (j + 1) * width, :]
        return lax.dot_general(hb_scr[...], wt, (((1,), (1,)), ((), ())), preferred_element_type=F32)

    def do_hb():
        xn = _rmsnorm_f32(x_ref[...], n1g_ref[...])
        hb_scr[...] = (xn * (1.0 + mod_ref[1:2, :]) + mod_ref[0:1, :]).astype(BF16)

    def do_vv(j):
        vv_ref[:, chunk(j)] = proj(o_vv, j).astype(BF16)

    def do_qk(j):
        qk = proj(o_q, j)
        qk_scr[:, chunk(j)] = qk * (B_DK ** -0.5) if (j + 1) * PROJ_NC <= QK_W else qk

    def do_z():
        al = proj(o_al, 0, LANES).astype(BF16)
        z_scr[...] = _dot(al, wab_ref[...]) + bab_ref[...]

    def do_v(j):
        vg_scr[:, chunk(j)] = _gelu(proj(o_v, j))

    def do_u_ga(j):
        ug_scr[:, chunk(j)] = _gelu(proj(o_u, j)).astype(BF16)
        sga_scr[:, chunk(j)] = jax.nn.sigmoid(proj(o_ga, j)).astype(BF16)

    def do_r_gb(j):
        r_ref[:, chunk(j)] = proj(o_r, j).astype(BF16)
        gb_ref[:, chunk(j)] = proj(o_gb, j).astype(BF16)

    def layer_norm_rows(c):
        rows = slice(c * A_CHUNK, (c + 1) * A_CHUNK)
        vg = vg_scr[rows, :]
        vc = vg - jnp.mean(vg, axis=-1, keepdims=True)
        vn_scr[rows, :] = (vc * lax.rsqrt(jnp.mean(vc * vc, axis=-1, keepdims=True) + EPS)
                           * lng_ref[...] + lnb_ref[...]).astype(BF16)

    def spatial_group(gi):
        gcols = slice(gi * gd, (gi + 1) * gd)
        rhs = jnp.concatenate([vn_scr[c * A_CHUNK:(c + 1) * A_CHUNK, gcols] for c in range(n_sp)], axis=1)
        mixed = _dot(ws_ref[gi], rhs)
        for c in range(n_sp):
            rows = slice(c * A_CHUNK, (c + 1) * A_CHUNK)
            a_scr[rows, gcols] = (ug_scr[rows, gcols].astype(F32)
                                  * (mixed[:, c * gd:(c + 1) * gd] + bsp_ref[:, gcols])).astype(BF16)

    def do_ya():
        ya_ref[...] = (sga_scr[...].astype(F32) * _dot(a_scr[...], wba_ref[...])).astype(BF16)

    def decay(rev):
        k_ref, a_ref = (kb_ref, ab_ref) if rev else (kf_ref, af_ref)
        zc = slice(QK_W if rev else 0, 2 * QK_W if rev else QK_W)
        g = _log_sigmoid(z_scr[:, zc]) / GATE_TEMP
        g_b = g.astype(BF16)
        tri = tri_ref[1 if rev else 0]
        for gi in range(tm // GLA_GROUP):
            rows = slice(gi * GLA_GROUP, (gi + 1) * GLA_GROUP)
            b_scr[rows, :] = _dot(tri, g_b[rows])
        for h in range(B_HEADS):
            cols = slice(h * B_DK, (h + 1) * B_DK)
            b = b_scr[:, cols]
            k_ref[cols, :] = jnp.transpose(qk_scr[:, QK_W + h * B_DK: QK_W + (h + 1) * B_DK]
                                           * jnp.exp(-b)).astype(BF16)
            if latent:
                q_ref = qb_ref if rev else qf_ref
                q_ref[:, cols] = (qk_scr[:, cols] * jnp.exp(b)).astype(BF16)
            tot = jnp.sum(g[:, cols].reshape(n_ch, GLA_CHUNK, B_DK), axis=1)
            tot = jnp.concatenate([tot, jnp.zeros((LANES - n_ch, B_DK), F32)], axis=0)
            a_t = jnp.exp(jnp.transpose(tot))
            for gi in range(tm // GLA_GROUP):
                a_ref[gi, cols, :] = a_t[:, gi * cpg:(gi + 1) * cpg]

    do_hb()
    for j in range(n_nc // 2 if latent else n_nc):
        do_vv(j)
    for j in range(0 if latent else n_qk // 2, n_qk):
        do_qk(j)
    do_z()
    if latent:
        for j in range(n_nc):
            do_v(j)
        for j in range(n_nc):
            do_u_ga(j)
        for c in range(n_sp):
            layer_norm_rows(c)
        for j in range(n_nc):
            do_r_gb(j)
        for gi in range(A_GROUPS):
            spatial_group(gi)
        do_ya()
    for rev in (False, True):
        decay(rev)
    if latent:
        _tie(hb_scr.at[0:16, 0:LANES], kb_ref.at[QK_W - 16:QK_W, 0:LANES])
        for j in range(n_nc // 2, n_nc):
            do_vv(j)


def _proj(latent, x, mod, n1g, w_t, wab, bab, tri, lng, lnb, ws, bsp, wba, tm):
    bsz, t, d = x.shape
    nt = t // tm
    gpt = tm // GLA_GROUP
    cpg = GLA_GROUP // GLA_CHUNK
    tok = lambda w: pl.BlockSpec((None, tm, w), lambda b, i: (b, i, 0))
    ktr = pl.BlockSpec((None, QK_W, tm), lambda b, i: (b, 0, i))
    dec = pl.BlockSpec((None, gpt, QK_W, cpg), lambda b, i: (b, i, 0, 0))
    sd = lambda w, dt: jax.ShapeDtypeStruct((bsz, t, w), dt)
    sktr = jax.ShapeDtypeStruct((bsz, QK_W, t), BF16)
    sdec = jax.ShapeDtypeStruct((bsz, t // GLA_GROUP, QK_W, cpg), F32)
    scratch = [pltpu.VMEM((tm, 2 * QK_W), F32), pltpu.VMEM((tm, 2 * QK_W), F32),
               pltpu.VMEM((tm, QK_W), F32), pltpu.VMEM((tm, d), BF16)]
    if latent:
        out_specs = [tok(d), tok(QK_W), ktr, tok(QK_W), ktr, tok(d), tok(d), tok(d), dec, dec]
        out_shape = [sd(d, BF16), sd(QK_W, BF16), sktr, sd(QK_W, BF16), sktr,
                     sd(d, BF16), sd(d, BF16), sd(d, BF16), sdec, sdec]
        scratch += [pltpu.VMEM((tm, d), BF16), pltpu.VMEM((tm, d), BF16), pltpu.VMEM((tm, d), F32),
                    pltpu.VMEM((tm, d), BF16), pltpu.VMEM((tm, d), BF16)]
    else:
        out_specs = [ktr, ktr, tok(d), dec, dec]
        out_shape = [sktr, sktr, sd(d, BF16), sdec, sdec]
    return pl.pallas_call(
        functools.partial(_proj_kernel, latent, d),
        grid=(bsz, nt),
        in_specs=[
            tok(d),
            pl.BlockSpec((None, 8, d), lambda b, i: (b, 0, 0)),
            _resident(n1g.shape), _resident(w_t.shape), _resident(wab.shape), _resident(bab.shape),
            _resident(tri.shape), _resident(lng.shape), _resident(lnb.shape), _resident(ws.shape),
            _resident(bsp.shape), _resident(wba.shape),
        ],
        out_specs=out_specs,
        out_shape=out_shape,
        scratch_shapes=scratch,
        compiler_params=pltpu.CompilerParams(
            dimension_semantics=("parallel", "parallel"),
            vmem_limit_bytes=V7X_VMEM_BYTES * 7 // 8),
        name="proj_lat" if latent else "proj_ctx",
    )(x, mod, n1g, w_t, wab, bab, tri, lng, lnb, ws, bsp, wba)


def _gla_kernel(t, tc, qf_ref, ktf_ref, qb_ref, ktb_ref, v_ref, af_ref, ab_ref,
                ktcf_ref, ktcb_ref, vc_ref, acf_ref, acb_ref, gain_ref, o_ref,
                part_ref, att_scr, oin_scr, x_scr):
    c = GLA_CHUNK
    gsz = GLA_GROUP
    cpg = gsz // c
    n_g = t // gsz
    r = lax.broadcasted_iota(jnp.int32, (gsz, gsz), 0)
    col = lax.broadcasted_iota(jnp.int32, (gsz, gsz), 1)
    blk = r - r % c

    def bcast(a_col):
        return jnp.broadcast_to(a_col, (B_DK, B_DV))

    def chunk_rows(v, n):
        parts = []
        if n:
            parts.append(jnp.zeros((n * c, v.shape[1]), v.dtype))
        parts.append(v[n * c:(n + 1) * c])
        if n < cpg - 1:
            parts.append(jnp.zeros(((cpg - 1 - n) * c, v.shape[1]), v.dtype))
        return jnp.concatenate(parts, axis=0)

    class Direction:
        def __init__(self, idx, q_ref, kt_ref, a_ref, reverse):
            self.idx, self.q_ref, self.kt_ref, self.a_ref, self.reverse = idx, q_ref, kt_ref, a_ref, reverse
            self.order = range(cpg - 1, -1, -1) if reverse else range(cpg)

        def group(self, p):
            return (n_g - 1 - p) if self.reverse else p

        def rows(self, p):
            return slice(self.group(p) * gsz, (self.group(p) + 1) * gsz)

        def scores(self, p, slot):
            rows = self.rows(p)
            att = _dot(self.q_ref[rows, :], self.kt_ref[:, rows])
            if self.reverse:
                att = jnp.where(col >= r, jnp.where(col < blk + c, att, 0.0), 0.0)
            else:
                att = jnp.where(col <= r, jnp.where(col >= blk, att, 0.0), 0.0)
            att_scr[self.idx, slot] = att.astype(BF16)

        def values(self, p, slot):
            rows = self.rows(p)
            v = v_ref[rows, :]
            kt = self.kt_ref[:, rows]
            oin_scr[self.idx, slot] = _dot(att_scr[self.idx, slot], v)
            for n in range(cpg):
                x_scr[self.idx, slot, n] = _dot(kt, chunk_rows(v, n))

        def state(self, p, slot, s, finish):
            rows = self.rows(p)
            g = self.group(p)
            q = self.q_ref[rows, :]
            o_inter = [None] * cpg
            for n in self.order:
                o_inter[n] = _dot(q[n * c:(n + 1) * c], s.astype(BF16))
                s = bcast(self.a_ref[g, :, n:n + 1]) * (s + x_scr[self.idx, slot, n])
            o = oin_scr[self.idx, slot] + jnp.concatenate(o_inter, axis=0)
            if finish:
                tot = o + part_ref[rows, :]
                y = tot * lax.rsqrt(jnp.mean(tot * tot, axis=-1, keepdims=True) + EPS) * gain_ref[...]
                o_ref[rows, :] = y.astype(o_ref.dtype)
            else:
                part_ref[rows, :] = o
            return s

    dirs = (Direction(0, qf_ref, ktf_ref, af_ref, False), Direction(1, qb_ref, ktb_ref, ab_ref, True))

    states = []
    ncg = tc // gsz
    for d, ktc_ref, ac_ref in zip(dirs, (ktcf_ref, ktcb_ref), (acf_ref, acb_ref)):
        s = jnp.zeros((B_DK, B_DV), F32)
        for gi in (range(ncg - 1, -1, -1) if d.reverse else range(ncg)):
            rows = slice(gi * gsz, (gi + 1) * gsz)
            kt = ktc_ref[:, rows]
            v = vc_ref[rows, :]
            for n in d.order:
                s = bcast(ac_ref[gi, :, n:n + 1]) * (s + _dot(kt, chunk_rows(v, n)))
        states.append(s)

    for d in dirs:
        d.scores(0, 0)
        d.scores(1, 1)
    for d in dirs:
        d.values(0, 0)

    half = n_g // 2
    for p in range(n_g - 2):
        for d in dirs:
            d.scores(p + 2, p % 2)
        for d in dirs:
            d.values(p + 1, (p + 1) % 2)
        states = [d.state(p, p % 2, s, p >= half) for d, s in zip(dirs, states)]

    for d in dirs:
        d.values(n_g - 1, (n_g - 1) % 2)
    states = [d.state(n_g - 2, n_g % 2, s, True) for d, s in zip(dirs, states)]
    for d, s in zip(dirs, states):
        d.state(n_g - 1, (n_g - 1) % 2, s, True)


def _gla(qf, ktf, qb, ktb, vv, af, ab, ktcf, ktcb, vc, acf, acb, gain):
    bsz, t, d = vv.shape
    tc = vc.shape[1]
    seq = lambda n, w: pl.BlockSpec((None, n, w), lambda b, h: (b, 0, h))
    seqt = lambda n: pl.BlockSpec((None, B_DK, n), lambda b, h: (b, h, 0))
    dec = lambda a: pl.BlockSpec((None, a.shape[1], B_DK, a.shape[3]), lambda b, h: (b, 0, h, 0))
    return pl.pallas_call(
        functools.partial(_gla_kernel, t, tc),
        grid=(bsz, B_HEADS),
        in_specs=[seq(t, B_DK), seqt(t), seq(t, B_DK), seqt(t), seq(t, B_DV),
                  dec(af), dec(ab), seqt(tc), seqt(tc), seq(tc, B_DV), dec(acf), dec(acb),
                  pl.BlockSpec((1, B_DV), lambda b, h: (0, h))],
        out_specs=seq(t, B_DV),
        out_shape=jax.ShapeDtypeStruct((bsz, t, d), BF16),
        scratch_shapes=[pltpu.VMEM((t, B_DV), F32),
                        pltpu.VMEM((2, 2, GLA_GROUP, GLA_GROUP), BF16),
                        pltpu.VMEM((2, 2, GLA_GROUP, B_DV), F32),
                        pltpu.VMEM((2, 2, GLA_GROUP // GLA_CHUNK, B_DK, B_DV), F32)],
        compiler_params=pltpu.CompilerParams(
            dimension_semantics=("parallel", "parallel"),
            vmem_limit_bytes=V7X_VMEM_BYTES * 3 // 4),
        name="gla",
    )(qf, ktf, qb, ktb, vv, af, ab, ktcf, ktcb, vc, acf, acb, gain)


def _tail_kernel(d_ff, x_ref, ya_ref, bl_ref, r_ref, gb_ref, mod_ref, wbb_ref, wo_ref, n2g_ref,
                 wfi_ref, wfo_ref, fng_ref, o_ref, f_scr):
    r = r_ref[...].astype(F32)
    b_out = (bl_ref[...].astype(F32) * (r * jax.nn.sigmoid(r))).astype(BF16)
    y = ya_ref[...].astype(F32) + jax.nn.sigmoid(gb_ref[...].astype(F32)) * _dot(b_out, wbb_ref[...])
    mix = _dot(y.astype(BF16), wo_ref[...])
    x1 = x_ref[...] + mod_ref[2:3, :] * mix
    h2 = (_rmsnorm_f32(x1, n2g_ref[...]) * (1.0 + mod_ref[4:5, :]) + mod_ref[3:4, :]).astype(BF16)
    for j in range(d_ff // FF_CHUNK):
        cols = slice(j * FF_CHUNK, (j + 1) * FF_CHUNK)
        a = _dot(h2, wfi_ref[:, cols])
        g = _dot(h2, wfi_ref[:, d_ff + j * FF_CHUNK: d_ff + (j + 1) * FF_CHUNK])
        f_scr[:, cols] = (g * jax.nn.sigmoid(g) * a).astype(BF16)
    x2 = x1 + mod_ref[5:6, :] * _dot(f_scr[...], wfo_ref[...])
    o_ref[...] = _rmsnorm_f32(x2, fng_ref[...])


def _tail(x, ya, bl, r, gb, mod, wbb, wo, n2g, wfi, wfo, fng, tm):
    bsz, t, d = x.shape
    d_ff = wfo.shape[0]
    tok = pl.BlockSpec((None, tm, d), lambda b, i: (b, i, 0))
    return pl.pallas_call(
        functools.partial(_tail_kernel, d_ff),
        grid=(bsz, t // tm),
        in_specs=[tok, tok, tok, tok, tok, pl.BlockSpec((None, 8, d), lambda b, i: (b, 0, 0)),
                  _resident(wbb.shape), _resident(wo.shape), _resident(n2g.shape),
                  _resident(wfi.shape), _resident(wfo.shape), _resident(fng.shape)],
        out_specs=tok,
        out_shape=jax.ShapeDtypeStruct((bsz, t, d), x.dtype),
        scratch_shapes=[pltpu.VMEM((tm, d_ff), BF16)],
        compiler_params=pltpu.CompilerParams(
            dimension_semantics=("parallel", "parallel"),
            vmem_limit_bytes=V7X_VMEM_BYTES * 7 // 8),
        name="tail",
    )(x, ya, bl, r, gb, mod, wbb, wo, n2g, wfi, wfo, fng)


def kernel(x, c, ctx, c_ctx, w_ada, b_ada, norm1_g, w_in, ln_v_g, ln_v_b, w_spatial, b_spatial,
           w_alpha_f, b_alpha_f, w_alpha_b, b_alpha_b, gla_norm_g, w_branch_a, w_branch_b,
           w_out, norm2_g, w_ffn_in, w_ffn_out, final_norm_g):
    bsz, t, d = x.shape
    tc = ctx.shape[1]
    depth = w_ada.shape[0]
    assert depth == 1 and d == B_HEADS * B_DV and d % A_GROUPS == 0
    assert t % (4 * GLA_GROUP) == 0 and t % PROJ_TM == 0 and PROJ_TM % GLA_GROUP == 0 and t % TAIL_TM == 0
    assert tc % A_CHUNK == 0 and tc % GLA_GROUP == 0
    l = 0

    rows = -(-(bsz + 1) // 8) * 8
    cc = jnp.zeros((rows, d), F32).at[:bsz].set(c).at[bsz].set(c_ctx)
    mod_all = _ada(cc, w_ada[l], b_ada[l]).reshape(rows, N_MOD, d)
    mod_all = jnp.pad(mod_all, ((0, 0), (0, 8 - N_MOD), (0, 0)))
    mod_lat = mod_all[:bsz]
    mod_ctx = jnp.broadcast_to(mod_all[bsz:bsz + 1], (bsz, 8, d))

    w_t = jnp.transpose(w_in[l]).astype(BF16)
    wab = jnp.zeros((LANES, 2 * QK_W), F32)
    wab = wab.at[:GATE_RANK, :QK_W].set(w_alpha_f[l]).at[GATE_RANK:2 * GATE_RANK, QK_W:].set(w_alpha_b[l])
    wab = wab.astype(BF16)
    bab = jnp.concatenate([b_alpha_f[l], b_alpha_b[l]]).reshape(1, 2 * QK_W)
    n1g = norm1_g[l].reshape(1, d)
    lng = ln_v_g[l].reshape(1, d)
    lnb = ln_v_b[l].reshape(1, d)
    ws = w_spatial[l].astype(BF16)
    bsp = jnp.repeat(jnp.transpose(b_spatial[l]), d // A_GROUPS, axis=1)
    wba = w_branch_a[l].astype(BF16)

    pos = jnp.arange(GLA_GROUP)
    same = (pos[:, None] // GLA_CHUNK) == (pos[None, :] // GLA_CHUNK)
    tri = jnp.stack([same & (pos[None, :] <= pos[:, None]), same & (pos[None, :] >= pos[:, None])]).astype(BF16)

    common = (n1g, w_t, wab, bab, tri, lng, lnb, ws, bsp, wba)
    ya, qf, ktf, qb, ktb, vv, r, gb, af, ab = _proj(True, x, mod_lat, *common, tm=PROJ_TM)
    ktcf, ktcb, vc, acf, acb = _proj(False, ctx, mod_ctx, *common, tm=tc)
    bl = _gla(qf, ktf, qb, ktb, vv, af, ab, ktcf, ktcb, vc, acf, acb,
              gla_norm_g[l].reshape(1, d))

    return _tail(x, ya, bl, r, gb, mod_lat, w_branch_b[l].astype(BF16), w_out[l].astype(BF16),
                 norm2_g[l].reshape(1, d), w_ffn_in[l].astype(BF16), w_ffn_out[l].astype(BF16),
                 final_norm_g.reshape(1, d), tm=TAIL_TM)
```

```python
import functools

import jax
import jax.numpy as jnp
from jax import lax
from jax.experimental import pallas as pl
from jax.experimental.pallas import tpu as pltpu

EPS = 1e-6
N_MOD = 6
A_GROUPS = 8
A_CHUNK = 128
B_HEADS = 4
B_DK = 128
B_DV = 256
QK_W = B_HEADS * B_DK
GATE_RANK = 16
GATE_TEMP = 16.0
GLA_CHUNK = 64
GLA_GROUP = 4 * GLA_CHUNK
LANES = 128
V7X_VMEM_BYTES = 64 * 1024 * 1024

PROJ_TM = 512
PROJ_NC = 256
TAIL_TM = 512
TAIL_RB = 256
FF_CHUNK = 256

F32 = jnp.float32
BF16 = jnp.bfloat16


def _dot(a, b):
    return jnp.dot(a, b, preferred_element_type=F32)


def _resident(shape):
    nd = len(shape)
    return pl.BlockSpec(shape, lambda *_: (0,) * nd, pipeline_mode=pl.Buffered(1))


def _rmsnorm_f32(x, g):
    return x * lax.rsqrt(jnp.mean(x * x, axis=-1, keepdims=True) + EPS) * g


def _ada_kernel(c_ref, w_ref, b_ref, o_ref):
    c = c_ref[...]
    s = (c * jax.nn.sigmoid(c)).astype(BF16)
    o_ref[...] = _dot(s, w_ref[...].astype(BF16)) + b_ref[...]


def _ada(cc, w_ada, b_ada):
    rows, d = cc.shape
    n = w_ada.shape[1]
    bn = 1024
    return pl.pallas_call(
        _ada_kernel,
        grid=(n // bn,),
        in_specs=[
            pl.BlockSpec((rows, d), lambda j: (0, 0)),
            pl.BlockSpec((d, bn), lambda j: (0, j)),
            pl.BlockSpec((1, bn), lambda j: (0, j)),
        ],
        out_specs=pl.BlockSpec((rows, bn), lambda j: (0, j)),
        out_shape=jax.ShapeDtypeStruct((rows, n), F32),
        name="ada",
    )(cc, w_ada, b_ada.reshape(1, n))


def _gelu(x):
    return 0.5 * x * (1.0 + lax.erf(x * (2.0 ** -0.5)))


def _log_sigmoid(z):
    return jnp.minimum(z, 0.0) - jnp.log(1.0 + jnp.exp(-jnp.abs(z)))


def _proj_kernel(latent, d, x_ref, mod_ref, n1g_ref, wt_ref, wab_ref, bab_ref, tri_ref,
                 lng_ref, lnb_ref, ws_ref, bsp_ref, wba_ref, *outs):
    if latent:
        (pack_ref, q2_ref, kt2_ref, a2_ref,
         qk_scr, z_scr, b_scr, hb_scr, al_scr, a_scr, ug_scr, vg_scr, vn_scr, sga_scr) = outs
        ya_ref, r_ref, gb_ref, vv_ref = (pack_ref.at[:, i * d:(i + 1) * d] for i in range(4))
        qf_ref, qb_ref = q2_ref.at[:, :QK_W], q2_ref.at[:, QK_W:]
    else:
        kt2_ref, vv_ref, a2_ref, qk_scr, z_scr, b_scr, hb_scr, al_scr = outs
    kf_ref, kb_ref = kt2_ref.at[:QK_W, :], kt2_ref.at[QK_W:, :]
    af_ref, ab_ref = a2_ref.at[:, :QK_W, :], a2_ref.at[:, QK_W:, :]
    tm = x_ref.shape[0]
    n_ch = tm // GLA_CHUNK
    cpg = GLA_GROUP // GLA_CHUNK
    n_nc = d // PROJ_NC
    n_sp = tm // A_CHUNK
    gd = d // A_GROUPS
    n_qk = 2 * QK_W // PROJ_NC
    o_u, o_v, o_q, o_vv, o_r = 0, d, 2 * d, 2 * d + 2 * QK_W, 3 * d + 2 * QK_W
    o_al = o_r + d
    o_ga, o_gb = o_al + 2 * GATE_RANK, o_al + 2 * GATE_RANK + d

    def chunk(j):
        return slice(j * PROJ_NC, (j + 1) * PROJ_NC)

    def proj(off, j, width=PROJ_NC):
        wt = wt_ref[off + j * width: off + (j + 1) * width, :]
        return lax.dot_general(hb_scr[...], wt, (((1,), (1,)), ((), ())), preferred_element_type=F32)

    def do_hb():
        xn = _rmsnorm_f32(x_ref[...], n1g_ref[...])
        hb_scr[...] = (xn * (1.0 + mod_ref[1:2, :]) + mod_ref[0:1, :]).astype(BF16)

    def do_vv(j):
        vv_ref[:, chunk(j)] = proj(o_vv, j).astype(BF16)

    def do_qk(j):
        qk = proj(o_q, j)
        qk_scr[:, chunk(j)] = qk * (B_DK ** -0.5) if (j + 1) * PROJ_NC <= QK_W else qk

    def do_al():
        al_scr[...] = proj(o_al, 0, LANES).astype(BF16)

    def do_z():
        z_scr[...] = _dot(al_scr[...], wab_ref[...]) + bab_ref[...]

    def do_v(j):
        vg_scr[:, chunk(j)] = _gelu(proj(o_v, j))

    def do_u_ga(j):
        ug_scr[:, chunk(j)] = _gelu(proj(o_u, j)).astype(BF16)
        sga_scr[:, chunk(j)] = jax.nn.sigmoid(proj(o_ga, j)).astype(BF16)

    def do_r_gb(j):
        r_ref[:, chunk(j)] = proj(o_r, j).astype(BF16)
        gb_ref[:, chunk(j)] = proj(o_gb, j).astype(BF16)

    def layer_norm_rows(c):
        rows = slice(c * A_CHUNK, (c + 1) * A_CHUNK)
        vg = vg_scr[rows, :]
        vc = vg - jnp.mean(vg, axis=-1, keepdims=True)
        vn_scr[rows, :] = (vc * lax.rsqrt(jnp.mean(vc * vc, axis=-1, keepdims=True) + EPS)
                           * lng_ref[...] + lnb_ref[...]).astype(BF16)

    def spatial_group(gi):
        gcols = slice(gi * gd, (gi + 1) * gd)
        rhs = jnp.concatenate([vn_scr[c * A_CHUNK:(c + 1) * A_CHUNK, gcols] for c in range(n_sp)], axis=1)
        mixed = _dot(ws_ref[gi], rhs)
        for c in range(n_sp):
            rows = slice(c * A_CHUNK, (c + 1) * A_CHUNK)
            a_scr[rows, gcols] = (ug_scr[rows, gcols].astype(F32)
                                  * (mixed[:, c * gd:(c + 1) * gd] + bsp_ref[:, gcols])).astype(BF16)

    def do_ya():
        ya_ref[...] = (sga_scr[...].astype(F32) * _dot(a_scr[...], wba_ref[...])).astype(BF16)

    def decay(rev):
        k_ref, a_ref = (kb_ref, ab_ref) if rev else (kf_ref, af_ref)
        zc = slice(QK_W if rev else 0, 2 * QK_W if rev else QK_W)
        g = _log_sigmoid(z_scr[:, zc]) / GATE_TEMP
        g_b = g.astype(BF16)
        tri = tri_ref[1 if rev else 0]
        for gi in range(tm // GLA_GROUP):
            rows = slice(gi * GLA_GROUP, (gi + 1) * GLA_GROUP)
            b_scr[rows, :] = _dot(tri, g_b[rows])
        for h in range(B_HEADS):
            cols = slice(h * B_DK, (h + 1) * B_DK)
            b = b_scr[:, cols]
            k_ref[cols, :] = jnp.transpose(qk_scr[:, QK_W + h * B_DK: QK_W + (h + 1) * B_DK]
                                           * jnp.exp(-b)).astype(BF16)
            if latent:
                q_ref = qb_ref if rev else qf_ref
                q_ref[:, cols] = (qk_scr[:, cols] * jnp.exp(b)).astype(BF16)
            tot = jnp.sum(g[:, cols].reshape(n_ch, GLA_CHUNK, B_DK), axis=1)
            tot = jnp.concatenate([tot, jnp.zeros((LANES - n_ch, B_DK), F32)], axis=0)
            a_t = jnp.exp(jnp.transpose(tot))
            for gi in range(tm // GLA_GROUP):
                a_ref[gi, cols, :] = a_t[:, gi * cpg:(gi + 1) * cpg]

    do_hb()
    do_al()
    for j in range(n_nc // 2 if latent else n_nc):
        do_vv(j)
    for j in range(0 if latent else n_qk // 2, n_qk):
        do_qk(j)
    do_z()
    if latent:
        for j in range(n_nc):
            do_v(j)
        for j in range(n_nc):
            do_u_ga(j)
        for c in range(n_sp):
            layer_norm_rows(c)
        for j in range(n_nc):
            do_r_gb(j)
        for gi in range(A_GROUPS):
            spatial_group(gi)
        do_ya()
    for rev in (False, True):
        decay(rev)
    if latent:
        for j in range(n_nc // 2, n_nc):
            do_vv(j)


def _proj(latent, x, mod, n1g, w_t, wab, bab, tri, lng, lnb, ws, bsp, wba, tm):
    bsz, t, d = x.shape
    nt = t // tm
    gpt = tm // GLA_GROUP
    cpg = GLA_GROUP // GLA_CHUNK
    tok = lambda w: pl.BlockSpec((None, tm, w), lambda b, i: (b, i, 0))
    ktr = pl.BlockSpec((None, 2 * QK_W, tm), lambda b, i: (b, 0, i))
    dec = pl.BlockSpec((None, gpt, 2 * QK_W, cpg), lambda b, i: (b, i, 0, 0))
    sd = lambda w, dt: jax.ShapeDtypeStruct((bsz, t, w), dt)
    sktr = jax.ShapeDtypeStruct((bsz, 2 * QK_W, t), BF16)
    sdec = jax.ShapeDtypeStruct((bsz, t // GLA_GROUP, 2 * QK_W, cpg), F32)
    scratch = [pltpu.VMEM((tm, 2 * QK_W), F32), pltpu.VMEM((tm, 2 * QK_W), F32),
               pltpu.VMEM((tm, QK_W), F32), pltpu.VMEM((tm, d), BF16), pltpu.VMEM((tm, LANES), BF16)]
    if latent:
        out_specs = [tok(4 * d), tok(2 * QK_W), ktr, dec]
        out_shape = [sd(4 * d, BF16), sd(2 * QK_W, BF16), sktr, sdec]
        scratch += [pltpu.VMEM((tm, d), BF16), pltpu.VMEM((tm, d), BF16), pltpu.VMEM((tm, d), F32),
                    pltpu.VMEM((tm, d), BF16), pltpu.VMEM((tm, d), BF16)]
    else:
        out_specs = [ktr, tok(d), dec]
        out_shape = [sktr, sd(d, BF16), sdec]
    return pl.pallas_call(
        functools.partial(_proj_kernel, latent, d),
        grid=(bsz, nt),
        in_specs=[
            tok(d),
            pl.BlockSpec((None, 8, d), lambda b, i: (b, 0, 0)),
            _resident(n1g.shape), _resident(w_t.shape), _resident(wab.shape), _resident(bab.shape),
            _resident(tri.shape), _resident(lng.shape), _resident(lnb.shape), _resident(ws.shape),
            _resident(bsp.shape), _resident(wba.shape),
        ],
        out_specs=out_specs,
        out_shape=out_shape,
        scratch_shapes=scratch,
        compiler_params=pltpu.CompilerParams(
            dimension_semantics=("parallel", "parallel"),
            vmem_limit_bytes=V7X_VMEM_BYTES * 7 // 8),
        name="proj_lat" if latent else "proj_ctx",
    )(x, mod, n1g, w_t, wab, bab, tri, lng, lnb, ws, bsp, wba)


def _gla_kernel(t, tc, qf_ref, ktf_ref, qb_ref, ktb_ref, v_ref, af_ref, ab_ref,
                ktcf_ref, ktcb_ref, vc_ref, acf_ref, acb_ref, gain_ref, o_ref,
                part_ref, att_scr, oin_scr, x_scr):
    c = GLA_CHUNK
    gsz = GLA_GROUP
    cpg = gsz // c
    n_g = t // gsz
    r = lax.broadcasted_iota(jnp.int32, (gsz, gsz), 0)
    col = lax.broadcasted_iota(jnp.int32, (gsz, gsz), 1)
    blk = r - r % c

    def bcast(a_col):
        return jnp.broadcast_to(a_col, (B_DK, B_DV))

    def chunk_rows(v, n):
        parts = []
        if n:
            parts.append(jnp.zeros((n * c, v.shape[1]), v.dtype))
        parts.append(v[n * c:(n + 1) * c])
        if n < cpg - 1:
            parts.append(jnp.zeros(((cpg - 1 - n) * c, v.shape[1]), v.dtype))
        return jnp.concatenate(parts, axis=0)

    class Direction:
        def __init__(self, idx, q_ref, kt_ref, a_ref, reverse):
            self.idx, self.q_ref, self.kt_ref, self.a_ref, self.reverse = idx, q_ref, kt_ref, a_ref, reverse
            self.order = range(cpg - 1, -1, -1) if reverse else range(cpg)

        def group(self, p):
            return (n_g - 1 - p) if self.reverse else p

        def rows(self, p):
            return slice(self.group(p) * gsz, (self.group(p) + 1) * gsz)

        def scores(self, p, slot):
            rows = self.rows(p)
            att = _dot(self.q_ref[rows, :], self.kt_ref[:, rows])
            if self.reverse:
                att = jnp.where(col >= r, jnp.where(col < blk + c, att, 0.0), 0.0)
            else:
                att = jnp.where(col <= r, jnp.where(col >= blk, att, 0.0), 0.0)
            att_scr[self.idx, slot] = att.astype(BF16)

        def values(self, p, slot):
            rows = self.rows(p)
            v = v_ref[rows, :]
            kt = self.kt_ref[:, rows]
            oin_scr[self.idx, slot] = _dot(att_scr[self.idx, slot], v)
            for n in range(cpg):
                x_scr[self.idx, slot, n] = _dot(kt, chunk_rows(v, n))

        def state(self, p, slot, s, finish):
            rows = self.rows(p)
            g = self.group(p)
            q = self.q_ref[rows, :]
            o_inter = [None] * cpg
            for n in self.order:
                o_inter[n] = _dot(q[n * c:(n + 1) * c], s.astype(BF16))
                s = bcast(self.a_ref[g, :, n:n + 1]) * (s + x_scr[self.idx, slot, n])
            o = oin_scr[self.idx, slot] + jnp.concatenate(o_inter, axis=0)
            if finish:
                tot = o + part_ref[rows, :]
                y = tot * lax.rsqrt(jnp.mean(tot * tot, axis=-1, keepdims=True) + EPS) * gain_ref[...]
                o_ref[rows, :] = y.astype(o_ref.dtype)
            else:
                part_ref[rows, :] = o
            return s

    dirs = (Direction(0, qf_ref, ktf_ref, af_ref, False), Direction(1, qb_ref, ktb_ref, ab_ref, True))

    states = []
    ncg = tc // gsz
    for d, ktc_ref, ac_ref in zip(dirs, (ktcf_ref, ktcb_ref), (acf_ref, acb_ref)):
        s = jnp.zeros((B_DK, B_DV), F32)
        for gi in (range(ncg - 1, -1, -1) if d.reverse else range(ncg)):
            rows = slice(gi * gsz, (gi + 1) * gsz)
            kt = ktc_ref[:, rows]
            v = vc_ref[rows, :]
            for n in d.order:
                s = bcast(ac_ref[gi, :, n:n + 1]) * (s + _dot(kt, chunk_rows(v, n)))
        states.append(s)

    for d in dirs:
        d.scores(0, 0)
        d.scores(1, 1)
    for d in dirs:
        d.values(0, 0)

    half = n_g // 2
    for p in range(n_g - 2):
        for d in dirs:
            d.scores(p + 2, p % 2)
        for d in dirs:
            d.values(p + 1, (p + 1) % 2)
        states = [d.state(p, p % 2, s, p >= half) for d, s in zip(dirs, states)]

    for d in dirs:
        d.values(n_g - 1, (n_g - 1) % 2)
    states = [d.state(n_g - 2, n_g % 2, s, True) for d, s in zip(dirs, states)]
    for d, s in zip(dirs, states):
        d.state(n_g - 1, (n_g - 1) % 2, s, True)


def _gla(pack, q2, kt2, a2, ktc2, vc, ac2, gain):
    bsz, t, d4 = pack.shape
    d = d4 // 4
    tc = vc.shape[1]
    seq = lambda n, w, off=0: pl.BlockSpec((None, n, w), lambda b, h: (b, 0, off + h))
    seqt = lambda n, off=0: pl.BlockSpec((None, B_DK, n), lambda b, h: (b, off + h, 0))
    dec = lambda a, off=0: pl.BlockSpec((None, a.shape[1], B_DK, a.shape[3]), lambda b, h: (b, 0, off + h, 0))
    bwd = B_HEADS
    return pl.pallas_call(
        functools.partial(_gla_kernel, t, tc),
        grid=(bsz, B_HEADS),
        in_specs=[seq(t, B_DK), seqt(t), seq(t, B_DK, bwd), seqt(t, bwd), seq(t, B_DV, 3 * d // B_DV),
                  dec(a2), dec(a2, bwd), seqt(tc), seqt(tc, bwd), seq(tc, B_DV), dec(ac2), dec(ac2, bwd),
                  pl.BlockSpec((1, B_DV), lambda b, h: (0, h))],
        out_specs=seq(t, B_DV),
        out_shape=jax.ShapeDtypeStruct((bsz, t, d), BF16),
        scratch_shapes=[pltpu.VMEM((t, B_DV), F32),
                        pltpu.VMEM((2, 2, GLA_GROUP, GLA_GROUP), BF16),
                        pltpu.VMEM((2, 2, GLA_GROUP, B_DV), F32),
                        pltpu.VMEM((2, 2, GLA_GROUP // GLA_CHUNK, B_DK, B_DV), F32)],
        compiler_params=pltpu.CompilerParams(
            dimension_semantics=("parallel", "parallel"),
            vmem_limit_bytes=V7X_VMEM_BYTES * 3 // 4),
        name="gla",
    )(q2, kt2, q2, kt2, pack, a2, a2, ktc2, ktc2, vc, ac2, ac2, gain)


def _tail_kernel(d_ff, x_ref, pack_ref, bl_ref, mod_ref, wbb_ref, wo_ref, n2g_ref,
                 wfi_ref, wfo_ref, fng_ref, o_ref, f_scr):
    tm, d = x_ref.shape
    ya_ref, r_ref, gb_ref = (pack_ref.at[:, i * d:(i + 1) * d] for i in range(3))
    n_ff = d_ff // FF_CHUNK
    blocks = [slice(rb * TAIL_RB, (rb + 1) * TAIL_RB) for rb in range(tm // TAIL_RB)]
    y, x1, h2 = {}, {}, {}

    def merge(rows):
        r = r_ref[rows, :].astype(F32)
        b_out = (bl_ref[rows, :].astype(F32) * (r * jax.nn.sigmoid(r))).astype(BF16)
        y[rows.start] = (ya_ref[rows, :].astype(F32)
                         + jax.nn.sigmoid(gb_ref[rows, :].astype(F32)) * _dot(b_out, wbb_ref[...]))

    def mix(rows):
        x1[rows.start] = x_ref[rows, :] + mod_ref[2:3, :] * _dot(y[rows.start].astype(BF16), wo_ref[...])
        h2[rows.start] = (_rmsnorm_f32(x1[rows.start], n2g_ref[...]) * (1.0 + mod_ref[4:5, :])
                          + mod_ref[3:4, :]).astype(BF16)

    def ffn_in(rows, j):
        cols = slice(j * FF_CHUNK, (j + 1) * FF_CHUNK)
        a = _dot(h2[rows.start], wfi_ref[:, cols])
        g = _dot(h2[rows.start], wfi_ref[:, d_ff + j * FF_CHUNK: d_ff + (j + 1) * FF_CHUNK])
        f_scr[rows, cols] = (g * jax.nn.sigmoid(g) * a).astype(BF16)

    def ffn_out(rows):
        x2 = x1[rows.start] + mod_ref[5:6, :] * _dot(f_scr[rows, :], wfo_ref[...])
        o_ref[rows, :] = _rmsnorm_f32(x2, fng_ref[...])

    merge(blocks[0])
    mix(blocks[0])
    for i, rows in enumerate(blocks):
        nxt = blocks[i + 1] if i + 1 < len(blocks) else None
        for j in range(n_ff):
            ffn_in(rows, j)
            if nxt is not None and j == n_ff // 3:
                merge(nxt)
            if nxt is not None and j == 2 * n_ff // 3:
                mix(nxt)
        ffn_out(rows)


def _tail(x, pack, bl, mod, wbb, wo, n2g, wfi, wfo, fng, tm):
    bsz, t, d = x.shape
    d_ff = wfo.shape[0]
    tok = pl.BlockSpec((None, tm, d), lambda b, i: (b, i, 0))
    return pl.pallas_call(
        functools.partial(_tail_kernel, d_ff),
        grid=(bsz, t // tm),
        in_specs=[tok, pl.BlockSpec((None, tm, 3 * d), lambda b, i: (b, i, 0)), tok,
                  pl.BlockSpec((None, 8, d), lambda b, i: (b, 0, 0)),
                  _resident(wbb.shape), _resident(wo.shape), _resident(n2g.shape),
                  _resident(wfi.shape), _resident(wfo.shape), _resident(fng.shape)],
        out_specs=tok,
        out_shape=jax.ShapeDtypeStruct((bsz, t, d), x.dtype),
        scratch_shapes=[pltpu.VMEM((tm, d_ff), BF16)],
        compiler_params=pltpu.CompilerParams(
            dimension_semantics=("parallel", "parallel"),
            vmem_limit_bytes=V7X_VMEM_BYTES * 7 // 8),
        name="tail",
    )(x, pack, bl, mod, wbb, wo, n2g, wfi, wfo, fng)


def kernel(x, c, ctx, c_ctx, w_ada, b_ada, norm1_g, w_in, ln_v_g, ln_v_b, w_spatial, b_spatial,
           w_alpha_f, b_alpha_f, w_alpha_b, b_alpha_b, gla_norm_g, w_branch_a, w_branch_b,
           w_out, norm2_g, w_ffn_in, w_ffn_out, final_norm_g):
    bsz, t, d = x.shape
    tc = ctx.shape[1]
    depth = w_ada.shape[0]
    assert depth == 1 and d == B_HEADS * B_DV and d % A_GROUPS == 0
    assert t % (4 * GLA_GROUP) == 0 and t % PROJ_TM == 0 and PROJ_TM % GLA_GROUP == 0 and t % TAIL_TM == 0
    assert tc % A_CHUNK == 0 and tc % GLA_GROUP == 0
    l = 0

    rows = -(-(bsz + 1) // 8) * 8
    cc = jnp.zeros((rows, d), F32).at[:bsz].set(c).at[bsz].set(c_ctx)
    mod_all = _ada(cc, w_ada[l], b_ada[l]).reshape(rows, N_MOD, d)
    mod_all = jnp.pad(mod_all, ((0, 0), (0, 8 - N_MOD), (0, 0)))
    mod_lat = mod_all[:bsz]
    mod_ctx = jnp.broadcast_to(mod_all[bsz:bsz + 1], (bsz, 8, d))

    w_t = jnp.transpose(w_in[l]).astype(BF16)
    wab = jnp.zeros((LANES, 2 * QK_W), F32)
    wab = wab.at[:GATE_RANK, :QK_W].set(w_alpha_f[l]).at[GATE_RANK:2 * GATE_RANK, QK_W:].set(w_alpha_b[l])
    wab = wab.astype(BF16)
    bab = jnp.concatenate([b_alpha_f[l], b_alpha_b[l]]).reshape(1, 2 * QK_W)
    n1g = norm1_g[l].reshape(1, d)
    lng = ln_v_g[l].reshape(1, d)
    lnb = ln_v_b[l].reshape(1, d)
    ws = w_spatial[l].astype(BF16)
    bsp = jnp.repeat(jnp.transpose(b_spatial[l]), d // A_GROUPS, axis=1)
    wba = w_branch_a[l].astype(BF16)

    pos = jnp.arange(GLA_GROUP)
    same = (pos[:, None] // GLA_CHUNK) == (pos[None, :] // GLA_CHUNK)
    tri = jnp.stack([same & (pos[None, :] <= pos[:, None]), same & (pos[None, :] >= pos[:, None])]).astype(BF16)

    common = (n1g, w_t, wab, bab, tri, lng, lnb, ws, bsp, wba)
    pack, q2, kt2, a2 = _proj(True, x, mod_lat, *common, tm=PROJ_TM)
    ktc2, vc, ac2 = _proj(False, ctx, mod_ctx, *common, tm=tc)
    bl = _gla(pack, q2, kt2, a2, ktc2, vc, ac2, gla_norm_g[l].reshape(1, d))

    return _tail(x, pack, bl, mod_lat, w_branch_b[l].astype(BF16), w_out[l].astype(BF16),
                 norm2_g[l].reshape(1, d), w_ffn_in[l].astype(BF16), w_ffn_out[l].astype(BF16),
                 final_norm_g.reshape(1, d), tm=TAIL_TM)
```

```python
import functools

import jax
import jax.numpy as jnp
from jax import lax
from jax.experimental import pallas as pl
from jax.experimental.pallas import tpu as pltpu

EPS = 1e-6
N_MOD = 6
A_GROUPS = 8
A_CHUNK = 128
B_HEADS = 4
B_DK = 128
B_DV = 256
QK_W = B_HEADS * B_DK
GATE_RANK = 16
GATE_TEMP = 16.0
GLA_CHUNK = 64
GLA_GROUP = 4 * GLA_CHUNK
LANES = 128
V7X_VMEM_BYTES = 64 * 1024 * 1024

ADA_BN = 1024
PROJ_TM = 512
PROJ_NC = 256
TAIL_TM = 512
TAIL_RB = 256
FF_CHUNK = 256

F32 = jnp.float32
BF16 = jnp.bfloat16


def _dot(a, b):
    return jnp.dot(a, b, preferred_element_type=F32)


def _resident(shape):
    nd = len(shape)
    return pl.BlockSpec(shape, lambda *_: (0,) * nd, pipeline_mode=pl.Buffered(1))


def _rmsnorm_f32(x, g):
    return x * lax.rsqrt(jnp.mean(x * x, axis=-1, keepdims=True) + EPS) * g


def _ada_kernel(c_ref, w_ref, b_ref, wt32_ref, wba32_ref, o_ref, wt_ref, wba_ref):
    c = c_ref[...]
    s = (c * jax.nn.sigmoid(c)).astype(BF16)
    o_ref[...] = _dot(s, w_ref[...].astype(BF16)) + b_ref[...]
    wt_ref[...] = wt32_ref[...].astype(BF16)
    wba_ref[...] = wba32_ref[...].astype(BF16)


def _ada(cc, w_ada, b_ada, w_t32, wba32):
    rows, d = cc.shape
    n = w_ada.shape[1]
    steps = n // ADA_BN
    nr = w_t32.shape[0] // steps
    nb = 256
    last = wba32.shape[0] // nb - 1
    assert w_t32.shape[0] == nr * steps and nr % 16 == 0 and wba32.shape[0] % nb == 0 and last < steps
    blk = lambda r, c, hi=None: pl.BlockSpec((r, c), lambda j: (j if hi is None else jnp.minimum(j, hi), 0))
    return pl.pallas_call(
        _ada_kernel,
        grid=(steps,),
        in_specs=[
            pl.BlockSpec((rows, d), lambda j: (0, 0)),
            pl.BlockSpec((d, ADA_BN), lambda j: (0, j)),
            pl.BlockSpec((1, ADA_BN), lambda j: (0, j)),
            blk(nr, w_t32.shape[1]), blk(nb, wba32.shape[1], last),
        ],
        out_specs=[pl.BlockSpec((rows, ADA_BN), lambda j: (0, j)), blk(nr, w_t32.shape[1]),
                   blk(nb, wba32.shape[1], last)],
        out_shape=[jax.ShapeDtypeStruct((rows, n), F32), jax.ShapeDtypeStruct(w_t32.shape, BF16),
                   jax.ShapeDtypeStruct(wba32.shape, BF16)],
        compiler_params=pltpu.CompilerParams(vmem_limit_bytes=V7X_VMEM_BYTES // 2),
        name="ada",
    )(cc, w_ada, b_ada.reshape(1, n), w_t32, wba32)


def _gelu(x):
    return 0.5 * x * (1.0 + lax.erf(x * (2.0 ** -0.5)))


def _log_sigmoid(z):
    return jnp.minimum(z, 0.0) - jnp.log(1.0 + jnp.exp(-jnp.abs(z)))


def _proj_kernel(latent, d, x_ref, mod_ref, n1g_ref, wt_ref, wab_ref, bab_ref, tri_ref,
                 lng_ref, lnb_ref, ws_ref, bsp_ref, wba_ref, *outs):
    if latent:
        (pack_ref, q2_ref, kt2_ref, a2_ref,
         qk_scr, z_scr, b_scr, hb_scr, al_scr, a_scr, ug_scr, vg_scr, vn_scr, sga_scr) = outs
        ya_ref, r_ref, gb_ref, vv_ref = (pack_ref.at[:, i * d:(i + 1) * d] for i in range(4))
        qf_ref, qb_ref = q2_ref.at[:, :QK_W], q2_ref.at[:, QK_W:]
    else:
        kt2_ref, vv_ref, a2_ref, qk_scr, z_scr, b_scr, hb_scr, al_scr = outs
    kf_ref, kb_ref = kt2_ref.at[:QK_W, :], kt2_ref.at[QK_W:, :]
    af_ref, ab_ref = a2_ref.at[:, :QK_W, :], a2_ref.at[:, QK_W:, :]
    tm = x_ref.shape[0]
    n_ch = tm // GLA_CHUNK
    cpg = GLA_GROUP // GLA_CHUNK
    n_nc = d // PROJ_NC
    n_sp = tm // A_CHUNK
    gd = d // A_GROUPS
    n_qk = 2 * QK_W // PROJ_NC
    o_u, o_v, o_q, o_vv, o_r = 0, d, 2 * d, 2 * d + 2 * QK_W, 3 * d + 2 * QK_W
    o_al = o_r + d
    o_ga, o_gb = o_al + 2 * GATE_RANK, o_al + 2 * GATE_RANK + d

    def chunk(j):
        return slice(j * PROJ_NC, (j + 1) * PROJ_NC)

    def proj(off, j, width=PROJ_NC):
        wt = wt_ref[off + j * width: off + (j + 1) * width, :]
        return lax.dot_general(hb_scr[...], wt, (((1,), (1,)), ((), ())), preferred_element_type=F32)

    def do_hb():
        xn = _rmsnorm_f32(x_ref[...], n1g_ref[...])
        hb_scr[...] = (xn * (1.0 + mod_ref[1:2, :]) + mod_ref[0:1, :]).astype(BF16)

    def do_vv(j):
        vv_ref[:, chunk(j)] = proj(o_vv, j).astype(BF16)

    def do_qk(j):
        qk = proj(o_q, j)
        qk_scr[:, chunk(j)] = qk * (B_DK ** -0.5) if (j + 1) * PROJ_NC <= QK_W else qk

    def do_al():
        al_scr[...] = proj(o_al, 0, LANES).astype(BF16)

    def do_z():
        z_scr[...] = _dot(al_scr[...], wab_ref[...]) + bab_ref[...]

    def do_v(j):
        vg_scr[:, chunk(j)] = _gelu(proj(o_v, j))

    def do_u_ga(j):
        ug_scr[:, chunk(j)] = _gelu(proj(o_u, j)).astype(BF16)
        sga_scr[:, chunk(j)] = jax.nn.sigmoid(proj(o_ga, j)).astype(BF16)

    def do_r_gb(j):
        r_ref[:, chunk(j)] = proj(o_r, j).astype(BF16)
        gb_ref[:, chunk(j)] = proj(o_gb, j).astype(BF16)

    def layer_norm_rows(c):
        rows = slice(c * A_CHUNK, (c + 1) * A_CHUNK)
        vg = vg_scr[rows, :]
        vc = vg - jnp.mean(vg, axis=-1, keepdims=True)
        vn_scr[rows, :] = (vc * lax.rsqrt(jnp.mean(vc * vc, axis=-1, keepdims=True) + EPS)
                           * lng_ref[...] + lnb_ref[...]).astype(BF16)

    def spatial_group(gi):
        gcols = slice(gi * gd, (gi + 1) * gd)
        rhs = jnp.concatenate([vn_scr[c * A_CHUNK:(c + 1) * A_CHUNK, gcols] for c in range(n_sp)], axis=1)
        mixed = _dot(ws_ref[gi], rhs)
        for c in range(n_sp):
            rows = slice(c * A_CHUNK, (c + 1) * A_CHUNK)
            a_scr[rows, gcols] = (ug_scr[rows, gcols].astype(F32)
                                  * (mixed[:, c * gd:(c + 1) * gd] + bsp_ref[:, gcols])).astype(BF16)

    def do_ya():
        ya_ref[...] = (sga_scr[...].astype(F32) * _dot(a_scr[...], wba_ref[...])).astype(BF16)

    def decay(rev):
        k_ref, a_ref = (kb_ref, ab_ref) if rev else (kf_ref, af_ref)
        zc = slice(QK_W if rev else 0, 2 * QK_W if rev else QK_W)
        g = _log_sigmoid(z_scr[:, zc]) / GATE_TEMP
        g_b = g.astype(BF16)
        tri = tri_ref[1 if rev else 0]
        for gi in range(tm // GLA_GROUP):
            rows = slice(gi * GLA_GROUP, (gi + 1) * GLA_GROUP)
            b_scr[rows, :] = _dot(tri, g_b[rows])
        for h in range(B_HEADS):
            cols = slice(h * B_DK, (h + 1) * B_DK)
            b = b_scr[:, cols]
            k_ref[cols, :] = jnp.transpose(qk_scr[:, QK_W + h * B_DK: QK_W + (h + 1) * B_DK]
                                           * jnp.exp(-b)).astype(BF16)
            if latent:
                q_ref = qb_ref if rev else qf_ref
                q_ref[:, cols] = (qk_scr[:, cols] * jnp.exp(b)).astype(BF16)
            tot = jnp.sum(g[:, cols].reshape(n_ch, GLA_CHUNK, B_DK), axis=1)
            tot = jnp.concatenate([tot, jnp.zeros((LANES - n_ch, B_DK), F32)], axis=0)
            a_t = jnp.exp(jnp.transpose(tot))
            for gi in range(tm // GLA_GROUP):
                a_ref[gi, cols, :] = a_t[:, gi * cpg:(gi + 1) * cpg]

    do_hb()
    do_al()
    for j in range(n_nc // 2 if latent else n_nc):
        do_vv(j)
    for j in range(0 if latent else n_qk // 2, n_qk):
        do_qk(j)
    do_z()
    if latent:
        for j in range(n_nc):
            do_v(j)
        for j in range(n_nc):
            do_u_ga(j)
        for c in range(n_sp):
            layer_norm_rows(c)
        for j in range(n_nc):
            do_r_gb(j)
        for gi in range(A_GROUPS):
            spatial_group(gi)
        do_ya()
    for rev in (False, True):
        decay(rev)
    if latent:
        for j in range(n_nc // 2, n_nc):
            do_vv(j)


def _proj(latent, x, mod, n1g, w_t, wab, bab, tri, lng, lnb, ws, bsp, wba, tm):
    bsz, t, d = x.shape
    nt = t // tm
    gpt = tm // GLA_GROUP
    cpg = GLA_GROUP // GLA_CHUNK
    tok = lambda w: pl.BlockSpec((None, tm, w), lambda b, i: (b, i, 0))
    ktr = pl.BlockSpec((None, 2 * QK_W, tm), lambda b, i: (b, 0, i))
    dec = pl.BlockSpec((None, gpt, 2 * QK_W, cpg), lambda b, i: (b, i, 0, 0))
    sd = lambda w, dt: jax.ShapeDtypeStruct((bsz, t, w), dt)
    sktr = jax.ShapeDtypeStruct((bsz, 2 * QK_W, t), BF16)
    sdec = jax.ShapeDtypeStruct((bsz, t // GLA_GROUP, 2 * QK_W, cpg), F32)
    scratch = [pltpu.VMEM((tm, 2 * QK_W), F32), pltpu.VMEM((tm, 2 * QK_W), F32),
               pltpu.VMEM((tm, QK_W), F32), pltpu.VMEM((tm, d), BF16), pltpu.VMEM((tm, LANES), BF16)]
    if latent:
        out_specs = [tok(4 * d), tok(2 * QK_W), ktr, dec]
        out_shape = [sd(4 * d, BF16), sd(2 * QK_W, BF16), sktr, sdec]
        scratch += [pltpu.VMEM((tm, d), BF16), pltpu.VMEM((tm, d), BF16), pltpu.VMEM((tm, d), F32),
                    pltpu.VMEM((tm, d), BF16), pltpu.VMEM((tm, d), BF16)]
    else:
        out_specs = [ktr, tok(d), dec]
        out_shape = [sktr, sd(d, BF16), sdec]
    return pl.pallas_call(
        functools.partial(_proj_kernel, latent, d),
        grid=(bsz, nt),
        in_specs=[
            tok(d),
            pl.BlockSpec((None, 8, d), lambda b, i: (b, 0, 0)),
            _resident(n1g.shape), _resident(w_t.shape), _resident(wab.shape), _resident(bab.shape),
            _resident(tri.shape), _resident(lng.shape), _resident(lnb.shape), _resident(ws.shape),
            _resident(bsp.shape), _resident(wba.shape),
        ],
        out_specs=out_specs,
        out_shape=out_shape,
        scratch_shapes=scratch,
        compiler_params=pltpu.CompilerParams(
            dimension_semantics=("parallel", "parallel"),
            vmem_limit_bytes=V7X_VMEM_BYTES * 7 // 8),
        name="proj_lat" if latent else "proj_ctx",
    )(x, mod, n1g, w_t, wab, bab, tri, lng, lnb, ws, bsp, wba)


def _gla_kernel(t, tc, qf_ref, ktf_ref, qb_ref, ktb_ref, v_ref, af_ref, ab_ref,
                ktcf_ref, ktcb_ref, vc_ref, acf_ref, acb_ref, gain_ref, *rest):
    n_w = (len(rest) - 5) // 2
    w32_refs, o_ref, w16_refs = rest[:n_w], rest[n_w], rest[n_w + 1:2 * n_w + 1]
    part_ref, att_scr, oin_scr, x_scr = rest[2 * n_w + 1:]
    for w32_ref, w16_ref in zip(w32_refs, w16_refs):
        w16_ref[...] = w32_ref[...].astype(BF16)

    c = GLA_CHUNK
    gsz = GLA_GROUP
    cpg = gsz // c
    n_g = t // gsz
    r = lax.broadcasted_iota(jnp.int32, (gsz, gsz), 0)
    col = lax.broadcasted_iota(jnp.int32, (gsz, gsz), 1)
    blk = r - r % c

    def bcast(a_col):
        return jnp.broadcast_to(a_col, (B_DK, B_DV))

    def chunk_rows(v, n):
        parts = []
        if n:
            parts.append(jnp.zeros((n * c, v.shape[1]), v.dtype))
        parts.append(v[n * c:(n + 1) * c])
        if n < cpg - 1:
            parts.append(jnp.zeros(((cpg - 1 - n) * c, v.shape[1]), v.dtype))
        return jnp.concatenate(parts, axis=0)

    class Direction:
        def __init__(self, idx, q_ref, kt_ref, a_ref, reverse):
            self.idx, self.q_ref, self.kt_ref, self.a_ref, self.reverse = idx, q_ref, kt_ref, a_ref, reverse
            self.order = range(cpg - 1, -1, -1) if reverse else range(cpg)

        def group(self, p):
            return (n_g - 1 - p) if self.reverse else p

        def rows(self, p):
            return slice(self.group(p) * gsz, (self.group(p) + 1) * gsz)

        def scores(self, p, slot):
            rows = self.rows(p)
            att = _dot(self.q_ref[rows, :], self.kt_ref[:, rows])
            if self.reverse:
                att = jnp.where(col >= r, jnp.where(col < blk + c, att, 0.0), 0.0)
            else:
                att = jnp.where(col <= r, jnp.where(col >= blk, att, 0.0), 0.0)
            att_scr[self.idx, slot] = att.astype(BF16)

        def values(self, p, slot):
            rows = self.rows(p)
            v = v_ref[rows, :]
            kt = self.kt_ref[:, rows]
            oin_scr[self.idx, slot] = _dot(att_scr[self.idx, slot], v)
            for n in range(cpg):
                x_scr[self.idx, slot, n] = _dot(kt, chunk_rows(v, n))

        def state(self, p, slot, s, finish):
            rows = self.rows(p)
            g = self.group(p)
            q = self.q_ref[rows, :]
            o_inter = [None] * cpg
            for n in self.order:
                o_inter[n] = _dot(q[n * c:(n + 1) * c], s.astype(BF16))
                s = bcast(self.a_ref[g, :, n:n + 1]) * (s + x_scr[self.idx, slot, n])
            o = oin_scr[self.idx, slot] + jnp.concatenate(o_inter, axis=0)
            if finish:
                tot = o + part_ref[rows, :]
                y = tot * lax.rsqrt(jnp.mean(tot * tot, axis=-1, keepdims=True) + EPS) * gain_ref[...]
                o_ref[rows, :] = y.astype(o_ref.dtype)
            else:
                part_ref[rows, :] = o
            return s

    dirs = (Direction(0, qf_ref, ktf_ref, af_ref, False), Direction(1, qb_ref, ktb_ref, ab_ref, True))

    states = []
    ncg = tc // gsz
    for d, ktc_ref, ac_ref in zip(dirs, (ktcf_ref, ktcb_ref), (acf_ref, acb_ref)):
        s = jnp.zeros((B_DK, B_DV), F32)
        for gi in (range(ncg - 1, -1, -1) if d.reverse else range(ncg)):
            rows = slice(gi * gsz, (gi + 1) * gsz)
            kt = ktc_ref[:, rows]
            v = vc_ref[rows, :]
            for n in d.order:
                s = bcast(ac_ref[gi, :, n:n + 1]) * (s + _dot(kt, chunk_rows(v, n)))
        states.append(s)

    for d in dirs:
        d.scores(0, 0)
        d.scores(1, 1)
    for d in dirs:
        d.values(0, 0)

    half = n_g // 2
    for p in range(n_g - 2):
        for d in dirs:
            d.scores(p + 2, p % 2)
        for d in dirs:
            d.values(p + 1, (p + 1) % 2)
        states = [d.state(p, p % 2, s, p >= half) for d, s in zip(dirs, states)]

    for d in dirs:
        d.values(n_g - 1, (n_g - 1) % 2)
    states = [d.state(n_g - 2, n_g % 2, s, True) for d, s in zip(dirs, states)]
    for d, s in zip(dirs, states):
        d.state(n_g - 1, (n_g - 1) % 2, s, True)


def _gla(pack, q2, kt2, a2, ktc2, vc, ac2, gain, weights32):
    bsz, t, d4 = pack.shape
    d = d4 // 4
    tc = vc.shape[1]
    seq = lambda n, w, off=0: pl.BlockSpec((None, n, w), lambda b, h: (b, 0, off + h))
    seqt = lambda n, off=0: pl.BlockSpec((None, B_DK, n), lambda b, h: (b, off + h, 0))
    dec = lambda a, off=0: pl.BlockSpec((None, a.shape[1], B_DK, a.shape[3]), lambda b, h: (b, 0, off + h, 0))
    bwd = B_HEADS
    steps = bsz * B_HEADS

    def cast_spec(w):
        n = w.shape[0]
        rows = min(r for r in range(16, n + 1, 16) if n % r == 0 and r * steps >= n)
        last = n // rows - 1
        return pl.BlockSpec((rows, w.shape[1]), lambda b, h: (jnp.minimum(b * B_HEADS + h, last), 0))

    cast_specs = [cast_spec(w) for w in weights32]
    outs = pl.pallas_call(
        functools.partial(_gla_kernel, t, tc),
        grid=(bsz, B_HEADS),
        in_specs=[seq(t, B_DK), seqt(t), seq(t, B_DK, bwd), seqt(t, bwd), seq(t, B_DV, 3 * d // B_DV),
                  dec(a2), dec(a2, bwd), seqt(tc), seqt(tc, bwd), seq(tc, B_DV), dec(ac2), dec(ac2, bwd),
                  pl.BlockSpec((1, B_DV), lambda b, h: (0, h))] + cast_specs,
        out_specs=[seq(t, B_DV)] + cast_specs,
        out_shape=[jax.ShapeDtypeStruct((bsz, t, d), BF16)]
        + [jax.ShapeDtypeStruct(w.shape, BF16) for w in weights32],
        scratch_shapes=[pltpu.VMEM((t, B_DV), F32),
                        pltpu.VMEM((2, 2, GLA_GROUP, GLA_GROUP), BF16),
                        pltpu.VMEM((2, 2, GLA_GROUP, B_DV), F32),
                        pltpu.VMEM((2, 2, GLA_GROUP // GLA_CHUNK, B_DK, B_DV), F32)],
        compiler_params=pltpu.CompilerParams(
            dimension_semantics=("arbitrary", "arbitrary"),
            vmem_limit_bytes=V7X_VMEM_BYTES * 3 // 4),
        name="gla",
    )(q2, kt2, q2, kt2, pack, a2, a2, ktc2, ktc2, vc, ac2, ac2, gain, *weights32)
    return outs[0], outs[1:]


def _tail_kernel(d_ff, x_ref, pack_ref, bl_ref, mod_ref, wbb_ref, wo_ref, n2g_ref,
                 wfi_ref, wfo_ref, fng_ref, o_ref, f_scr):
    tm, d = x_ref.shape
    ya_ref, r_ref, gb_ref = (pack_ref.at[:, i * d:(i + 1) * d] for i in range(3))
    n_ff = d_ff // FF_CHUNK
    blocks = [slice(rb * TAIL_RB, (rb + 1) * TAIL_RB) for rb in range(tm // TAIL_RB)]
    y, x1, h2 = {}, {}, {}

    def merge(rows):
        r = r_ref[rows, :].astype(F32)
        b_out = (bl_ref[rows, :].astype(F32) * (r * jax.nn.sigmoid(r))).astype(BF16)
        y[rows.start] = (ya_ref[rows, :].astype(F32)
                         + jax.nn.sigmoid(gb_ref[rows, :].astype(F32)) * _dot(b_out, wbb_ref[...]))

    def mix(rows):
        x1[rows.start] = x_ref[rows, :] + mod_ref[2:3, :] * _dot(y[rows.start].astype(BF16), wo_ref[...])
        h2[rows.start] = (_rmsnorm_f32(x1[rows.start], n2g_ref[...]) * (1.0 + mod_ref[4:5, :])
                          + mod_ref[3:4, :]).astype(BF16)

    def ffn_in(rows, j):
        cols = slice(j * FF_CHUNK, (j + 1) * FF_CHUNK)
        a = _dot(h2[rows.start], wfi_ref[:, cols])
        g = _dot(h2[rows.start], wfi_ref[:, d_ff + j * FF_CHUNK: d_ff + (j + 1) * FF_CHUNK])
        f_scr[rows, cols] = (g * jax.nn.sigmoid(g) * a).astype(BF16)

    def ffn_out(rows):
        x2 = x1[rows.start] + mod_ref[5:6, :] * _dot(f_scr[rows, :], wfo_ref[...])
        o_ref[rows, :] = _rmsnorm_f32(x2, fng_ref[...])

    merge(blocks[0])
    mix(blocks[0])
    for i, rows in enumerate(blocks):
        nxt = blocks[i + 1] if i + 1 < len(blocks) else None
        for j in range(n_ff):
            ffn_in(rows, j)
            if nxt is not None and j == n_ff // 3:
                merge(nxt)
            if nxt is not None and j == 2 * n_ff // 3:
                mix(nxt)
        ffn_out(rows)


def _tail(x, pack, bl, mod, wbb, wo, n2g, wfi, wfo, fng, tm):
    bsz, t, d = x.shape
    d_ff = wfo.shape[0]
    tok = pl.BlockSpec((None, tm, d), lambda b, i: (b, i, 0))
    return pl.pallas_call(
        functools.partial(_tail_kernel, d_ff),
        grid=(bsz, t // tm),
        in_specs=[tok, pl.BlockSpec((None, tm, 3 * d), lambda b, i: (b, i, 0)), tok,
                  pl.BlockSpec((None, 8, d), lambda b, i: (b, 0, 0)),
                  _resident(wbb.shape), _resident(wo.shape), _resident(n2g.shape),
                  _resident(wfi.shape), _resident(wfo.shape), _resident(fng.shape)],
        out_specs=tok,
        out_shape=jax.ShapeDtypeStruct((bsz, t, d), x.dtype),
        scratch_shapes=[pltpu.VMEM((tm, d_ff), BF16)],
        compiler_params=pltpu.CompilerParams(
            dimension_semantics=("parallel", "parallel"),
            vmem_limit_bytes=V7X_VMEM_BYTES * 7 // 8),
        name="tail",
    )(x, pack, bl, mod, wbb, wo, n2g, wfi, wfo, fng)


def kernel(x, c, ctx, c_ctx, w_ada, b_ada, norm1_g, w_in, ln_v_g, ln_v_b, w_spatial, b_spatial,
           w_alpha_f, b_alpha_f, w_alpha_b, b_alpha_b, gla_norm_g, w_branch_a, w_branch_b,
           w_out, norm2_g, w_ffn_in, w_ffn_out, final_norm_g):
    bsz, t, d = x.shape
    tc = ctx.shape[1]
    depth = w_ada.shape[0]
    assert depth == 1 and d == B_HEADS * B_DV and d % A_GROUPS == 0
    assert t % (4 * GLA_GROUP) == 0 and t % PROJ_TM == 0 and PROJ_TM % GLA_GROUP == 0 and t % TAIL_TM == 0
    assert tc % A_CHUNK == 0 and tc % GLA_GROUP == 0
    l = 0

    rows = -(-(bsz + 1) // 8) * 8
    cc = jnp.zeros((rows, d), F32).at[:bsz].set(c).at[bsz].set(c_ctx)
    mod_all, w_t, wba = _ada(cc, w_ada[l], b_ada[l], jnp.transpose(w_in[l]), w_branch_a[l])
    mod_all = mod_all.reshape(rows, N_MOD, d)
    mod_all = jnp.pad(mod_all, ((0, 0), (0, 8 - N_MOD), (0, 0)))
    mod_lat = mod_all[:bsz]
    mod_ctx = jnp.broadcast_to(mod_all[bsz:bsz + 1], (bsz, 8, d))

    wab = jnp.zeros((LANES, 2 * QK_W), F32)
    wab = wab.at[:GATE_RANK, :QK_W].set(w_alpha_f[l]).at[GATE_RANK:2 * GATE_RANK, QK_W:].set(w_alpha_b[l])
    wab = wab.astype(BF16)
    bab = jnp.concatenate([b_alpha_f[l], b_alpha_b[l]]).reshape(1, 2 * QK_W)
    n1g = norm1_g[l].reshape(1, d)
    lng = ln_v_g[l].reshape(1, d)
    lnb = ln_v_b[l].reshape(1, d)
    ws = w_spatial[l].astype(BF16)
    bsp = jnp.repeat(jnp.transpose(b_spatial[l]), d // A_GROUPS, axis=1)

    pos = jnp.arange(GLA_GROUP)
    same = (pos[:, None] // GLA_CHUNK) == (pos[None, :] // GLA_CHUNK)
    tri = jnp.stack([same & (pos[None, :] <= pos[:, None]), same & (pos[None, :] >= pos[:, None])]).astype(BF16)

    common = (n1g, w_t, wab, bab, tri, lng, lnb, ws, bsp, wba)
    pack, q2, kt2, a2 = _proj(True, x, mod_lat, *common, tm=PROJ_TM)
    ktc2, vc, ac2 = _proj(False, ctx, mod_ctx, *common, tm=tc)
    bl, (wbb, wo, wfi, wfo) = _gla(pack, q2, kt2, a2, ktc2, vc, ac2, gla_norm_g[l].reshape(1, d),
                                   (w_branch_b[l], w_out[l], w_ffn_in[l], w_ffn_out[l]))

    return _tail(x, pack, bl, mod_lat, wbb, wo, norm2_g[l].reshape(1, d), wfi, wfo,
                 final_norm_g.reshape(1, d), tm=TAIL_TM)
```
